```python
import jax
import jax.numpy as jnp
from jax import lax
import numpy as np

D_MODEL = 1024
BATCH = 2
SEQ = 16384
DEPTH = 4

N_MIXERS = 4
ALPHA = (2 * DEPTH) ** 0.25
BETA = (8 * DEPTH) ** -0.25
LN_EPS = 1e-5
D_FF = 2816
FFN_HALF = 0.5

RET_HEADS = 4
RET_DK = D_MODEL // RET_HEADS
RET_DV = 2 * RET_DK
RET_CHUNK = 128
ROPE_BASE = 10000.0
RET_GN_EPS = 1e-5

S5_WIDTH = D_MODEL
S5_GROUP = 16
S5_GROUPS = S5_WIDTH // S5_GROUP
S5_STATE = 64
S5_CHUNK = 128
S5_DT_MIN = 1e-3
S5_DT_MAX = 1e-1

RW_HEAD = 64
RW_HEADS = D_MODEL // RW_HEAD
RW_DECAY_LORA = 64
RW_AAA_LORA = 64
RW_GATE_LORA = 128
RW_GN_EPS = 64e-5
RW_N_MIX = 6

LRU_WIDTH = D_MODEL
LRU_BLOCKS = 4
LRU_BLOCK = LRU_WIDTH // LRU_BLOCKS
CONV_WIDTH = 4
LRU_C = 8.0
LRU_A_MIN = 0.9
LRU_A_MAX = 0.999

kernel_name = 'hybrid_ret_s5_rwkv7_rglru_trunk'


def _n_occ(m):
    return len(range(m, DEPTH, N_MIXERS))


def layer_norm(x, g, b):
    xf = x.astype(jnp.float32)
    mu = jnp.mean(xf, -1, keepdims=True)
    var = jnp.mean(jnp.square(xf - mu), -1, keepdims=True)
    return ((xf - mu) * lax.rsqrt(var + LN_EPS) * g + b).astype(x.dtype)


def head_norm(y, eps):
    mu = jnp.mean(y, -1, keepdims=True)
    var = jnp.mean(jnp.square(y - mu), -1, keepdims=True)
    return (y - mu) * lax.rsqrt(var + eps)


def swiglu(x, w1, w3, w2):
    return (jax.nn.silu(x @ w1) * (x @ w3)) @ w2


def rotary(t):
    T, d = t.shape[1], t.shape[-1]
    half = d // 2
    inv = ROPE_BASE ** (-jnp.arange(half, dtype=jnp.float32) / half)
    ang = jnp.arange(T, dtype=jnp.float32)[:, None] * inv[None, :]
    cos = jnp.cos(ang)[None, :, None, :]
    sin = jnp.sin(ang)[None, :, None, :]
    t1, t2 = t[..., :half], t[..., half:]
    return jnp.concatenate([t1 * cos - t2 * sin, t1 * sin + t2 * cos], axis=-1)


def retention(x, w_in, w_out):
    B, T, _ = x.shape
    H, DK, DV, C = RET_HEADS, RET_DK, RET_DV, RET_CHUNK
    nc = T // C
    q, k, v, g = jnp.split(x @ w_in, [H * DK, 2 * H * DK, 2 * H * DK + H * DV], axis=-1)
    q = rotary(q.reshape(B, T, H, DK).astype(jnp.float32))
    k = rotary(k.reshape(B, T, H, DK).astype(jnp.float32)) * (DK ** -0.5)
    v = v.reshape(B, T, H, DV).astype(jnp.float32)
    log_gamma = jnp.log1p(-jnp.power(2.0, -5.0 - jnp.arange(H, dtype=jnp.float32)))
    pos = jnp.arange(C, dtype=jnp.float32)
    rel = pos[:, None] - pos[None, :]
    inner_decay = jnp.where(rel >= 0, jnp.exp(jnp.maximum(rel, 0.0)[None] * log_gamma[:, None, None]), 0.0)
    q_decay = jnp.exp((pos + 1.0)[:, None] * log_gamma)[..., None]
    k_decay = jnp.exp((C - 1.0 - pos)[:, None] * log_gamma)[..., None]
    chunk_decay = jnp.exp(C * log_gamma)[:, None, None]

    def to_chunks(t):
        return t.reshape(B, nc, C, H, t.shape[-1]).swapaxes(0, 1)

    def chunk_step(R, qkv):
        qc, kc, vc = qkv
        s = jnp.einsum('bihd,bjhd->bhij', qc, kc) * inner_decay
        inner = jnp.einsum('bhij,bjhe->bihe', s, vc)
        cross = jnp.einsum('bihd,bhde->bihe', qc * q_decay, R)
        R = R * chunk_decay + jnp.einsum('bjhd,bjhe->bhde', kc * k_decay, vc)
        return R, inner + cross

    R0 = jnp.zeros((B, H, DK, DV), jnp.float32)
    _, o = lax.scan(chunk_step, R0, (to_chunks(q), to_chunks(k), to_chunks(v)))
    o = head_norm(o.swapaxes(0, 1).reshape(B, T, H, DV), RET_GN_EPS).reshape(B, T, H * DV)
    o = jax.nn.silu(g.astype(jnp.float32)) * o
    return o.astype(x.dtype) @ w_out


def _complex_combine(e1, e2):
    a1r, a1i, b1r, b1i = e1
    a2r, a2i, b2r, b2i = e2
    return (a2r * a1r - a2i * a1i,
            a2r * a1i + a2i * a1r,
            a2r * b1r - a2i * b1i + b2r,
            a2r * b1i + a2i * b1r + b2i)


def s5_mixer(x, w_in, a_re, a_im, b_re, b_im, c_re, c_im, d_skip, log_step, w_glu, w_out):
    B, T, _ = x.shape
    G, P, N, C = S5_GROUPS, S5_GROUP, S5_STATE, S5_CHUNK
    nc = T // C
    f32 = jnp.float32
    a_re, a_im = a_re.astype(f32), a_im.astype(f32)
    b_re, b_im = b_re.astype(f32), b_im.astype(f32)
    c_re, c_im = c_re.astype(f32), c_im.astype(f32)
    u = (x @ w_in).astype(f32)
    dt = jnp.exp(log_step.astype(f32))[:, None]
    mag = jnp.exp(dt * a_re)
    abar_re = mag * jnp.cos(dt * a_im)
    abar_im = mag * jnp.sin(dt * a_im)
    den = a_re * a_re + a_im * a_im
    f_re = ((abar_re - 1.0) * a_re + abar_im * a_im) / den
    f_im = (abar_im * a_re - (abar_re - 1.0) * a_im) / den
    bb_re = f_re[..., None] * b_re - f_im[..., None] * b_im
    bb_im = f_re[..., None] * b_im + f_im[..., None] * b_re
    a_blk_re = jnp.broadcast_to(abar_re, (B, C, G, N))
    a_blk_im = jnp.broadcast_to(abar_im, (B, C, G, N))

    def chunk_step(h, u_blk):
        h_re, h_im = h
        bu_re = jnp.einsum('bcgp,gnp->bcgn', u_blk, bb_re)
        bu_im = jnp.einsum('bcgp,gnp->bcgn', u_blk, bb_im)
        p_re, p_im, s_re, s_im = lax.associative_scan(
            _complex_combine, (a_blk_re, a_blk_im, bu_re, bu_im), axis=1)
        st_re = p_re * h_re[:, None] - p_im * h_im[:, None] + s_re
        st_im = p_re * h_im[:, None] + p_im * h_re[:, None] + s_im
        y = jnp.einsum('bcgn,gpn->bcgp', st_re, c_re) - jnp.einsum('bcgn,gpn->bcgp', st_im, c_im)
        return (st_re[:, -1], st_im[:, -1]), y

    h0 = (jnp.zeros((B, G, N), f32), jnp.zeros((B, G, N), f32))
    u_chunks = u.reshape(B, nc, C, G, P).swapaxes(0, 1)
    _, y = lax.scan(chunk_step, h0, u_chunks)
    y = y.swapaxes(0, 1).reshape(B, T, G * P) + d_skip.astype(f32) * u
    act = jax.nn.gelu(y).astype(x.dtype)
    z = act * jax.nn.sigmoid(act @ w_glu)
    return z @ w_out


def rwkv7_mixer(x, mu, w_r, w_k, w_v, w0, w1, w2, a0, a1, a2, g1, g2,
                k_k, k_a, r_k, lnx_g, lnx_b, w_o):
    B, T, D = x.shape
    H, N = RW_HEADS, RW_HEAD
    f32 = jnp.float32
    xx = jnp.pad(x, ((0, 0), (1, 0), (0, 0)))[:, :-1] - x
    xr, xw, xk, xv, xa, xg = (x + xx * mu[i] for i in range(RW_N_MIX))
    r = (xr @ w_r).astype(f32)
    k = (xk @ w_k).astype(f32)
    v = (xv @ w_v).astype(f32)
    w = -jax.nn.softplus(-(w0 + jnp.tanh(xw @ w1) @ w2).astype(f32)) - 0.5
    decay = jnp.exp(-jnp.exp(w))
    a = jax.nn.sigmoid((a0 + (xa @ a1) @ a2).astype(f32))
    g = (jax.nn.sigmoid(xg @ g1) @ g2).astype(f32)
    heads = lambda t: t.reshape(B, T, H, N)
    r, k, v, decay, a = heads(r), heads(k), heads(v), heads(decay), heads(a)
    kk = k * k_k.astype(f32).reshape(H, N)
    kk = kk / jnp.maximum(jnp.sqrt(jnp.sum(kk * kk, -1, keepdims=True)), 1e-12)
    k = k * (1.0 + (a - 1.0) * k_a.astype(f32).reshape(H, N))

    def step(S, inp):
        r_t, w_t, k_t, v_t, kk_t, a_t = inp
        sa = jnp.einsum('bhvk,bhk->bhv', S, -kk_t)
        S = (S * w_t[:, :, None, :] + sa[..., None] * (kk_t * a_t)[:, :, None, :]
             + v_t[..., None] * k_t[:, :, None, :])
        return S, jnp.einsum('bhvk,bhk->bhv', S, r_t)

    xs = tuple(t.swapaxes(0, 1) for t in (r, decay, k, v, kk, a))
    S0 = jnp.zeros((B, H, N, N), f32)
    _, y = lax.scan(step, S0, xs)
    y = y.swapaxes(0, 1)
    y = head_norm(y, RW_GN_EPS) * lnx_g.astype(f32).reshape(H, N) + lnx_b.astype(f32).reshape(H, N)
    y = y + jnp.sum(r * k * r_k.astype(f32), -1, keepdims=True) * v
    y = (y.reshape(B, T, D) * g).astype(x.dtype)
    return y @ w_o


def _real_combine(e1, e2):
    a1, b1 = e1
    a2, b2 = e2
    return a2 * a1, a2 * b1 + b2


def rglru_block(x, w_in, conv_w, conv_b, w_a, b_a, w_x, b_x, lam, w_out):
    B, T, _ = x.shape
    f32 = jnp.float32
    gate, xr = jnp.split(x @ w_in, 2, axis=-1)
    gate = jax.nn.gelu(gate)
    xr = lax.conv_general_dilated(
        xr, conv_w.astype(xr.dtype)[:, None, :], window_strides=(1,),
        padding=[(CONV_WIDTH - 1, 0)], dimension_numbers=('NWC', 'WIO', 'NWC'),
        feature_group_count=LRU_WIDTH) + conv_b
    xb = xr.reshape(B, T, LRU_BLOCKS, LRU_BLOCK)
    gr = jnp.einsum('btki,kij->btkj', xb, w_a).reshape(B, T, LRU_WIDTH) + b_a
    gi = jnp.einsum('btki,kij->btkj', xb, w_x).reshape(B, T, LRU_WIDTH) + b_x
    r_t = jax.nn.sigmoid(gr.astype(f32))
    i_t = jax.nn.sigmoid(gi.astype(f32))
    log_a = -LRU_C * r_t * jax.nn.softplus(-lam.astype(f32))
    a_t = jnp.exp(log_a)
    inp = jnp.sqrt(-jnp.expm1(2.0 * log_a)) * (i_t * xr.astype(f32))
    _, h = lax.associative_scan(_real_combine, (a_t, inp), axis=1)
    y = (h * gate.astype(f32)).astype(x.dtype)
    return y @ w_out


def setup_inputs(seed: int = 0) -> dict:
    key = jax.random.key(seed)
    ks = iter(jax.random.split(key, 64))
    f32 = jnp.float32
    D = D_MODEL

    def nrm(shape, scale):
        return scale * jax.random.normal(next(ks), shape, f32)

    def unif(shape, lo, hi):
        return jax.random.uniform(next(ks), shape, f32, lo, hi)

    nA, nB, nC, nD = (_n_occ(m) for m in range(N_MIXERS))
    x = nrm((BATCH, SEQ, D), 1.0)
    ln_g = 1.0 + nrm((DEPTH, 3, D), 0.02)
    ln_b = nrm((DEPTH, 3, D), 0.02)
    ffn_w1 = nrm((DEPTH, 2, D, D_FF), D ** -0.5)
    ffn_w3 = nrm((DEPTH, 2, D, D_FF), D ** -0.5)
    ffn_w2 = nrm((DEPTH, 2, D_FF, D), BETA * D_FF ** -0.5)

    ret_w_in = nrm((nA, D, 2 * RET_HEADS * RET_DK + 2 * RET_HEADS * RET_DV), D ** -0.5)
    ret_w_out = nrm((nA, RET_HEADS * RET_DV, D), BETA * (RET_HEADS * RET_DV) ** -0.5)

    G, P, N = S5_GROUPS, S5_GROUP, S5_STATE
    s5_w_in = nrm((nB, D, S5_WIDTH), D ** -0.5)
    s5_a_re = -0.5 + nrm((nB, G, N), 0.01)
    s5_a_im = jnp.pi * jnp.arange(N, dtype=f32)[None, None, :] + nrm((nB, G, N), 0.01)
    s5_b_re = nrm((nB, G, N, P), (2 * P) ** -0.5)
    s5_b_im = nrm((nB, G, N, P), (2 * P) ** -0.5)
    s5_c_re = nrm((nB, G, P, N), N ** -0.5)
    s5_c_im = nrm((nB, G, P, N), N ** -0.5)
    s5_d = nrm((nB, S5_WIDTH), 1.0)
    s5_log_step = unif((nB, G), float(np.log(S5_DT_MIN)), float(np.log(S5_DT_MAX)))
    s5_w_glu = nrm((nB, S5_WIDTH, S5_WIDTH), S5_WIDTH ** -0.5)
    s5_w_out = nrm((nB, S5_WIDTH, D), BETA * S5_WIDTH ** -0.5)

    rw_mu = unif((nC, RW_N_MIX, D), 0.0, 1.0)
    rw_w_r = nrm((nC, D, D), D ** -0.5)
    rw_w_k = nrm((nC, D, D), D ** -0.5)
    rw_w_v = nrm((nC, D, D), D ** -0.5)
    decay_speed = -6.0 + 5.0 * (jnp.arange(D, dtype=f32) / (D - 1)) ** 0.85 + 0.5
    rw_w0 = decay_speed[None, :] + nrm((nC, D), 0.01)
    rw_w1 = nrm((nC, D, RW_DECAY_LORA), D ** -0.5)
    rw_w2 = nrm((nC, RW_DECAY_LORA, D), 0.1 * RW_DECAY_LORA ** -0.5)
    rw_a0 = nrm((nC, D), 0.1)
    rw_a1 = nrm((nC, D, RW_AAA_LORA), D ** -0.5)
    rw_a2 = nrm((nC, RW_AAA_LORA, D), 0.1 * RW_AAA_LORA ** -0.5)
    rw_g1 = nrm((nC, D, RW_GATE_LORA), D ** -0.5)
    rw_g2 = nrm((nC, RW_GATE_LORA, D), RW_GATE_LORA ** -0.5)
    rw_k_k = 0.85 + nrm((nC, D), 0.02)
    rw_k_a = 1.0 + nrm((nC, D), 0.02)
    rw_r_k = nrm((nC, RW_HEADS, RW_HEAD), 0.1)
    rw_lnx_g = 1.0 + nrm((nC, D), 0.02)
    rw_lnx_b = nrm((nC, D), 0.02)
    rw_w_o = nrm((nC, D, D), BETA * D ** -0.5)

    lru_w_in = nrm((nD, D, 2 * LRU_WIDTH), D ** -0.5)
    lru_conv_w = nrm((nD, CONV_WIDTH, LRU_WIDTH), CONV_WIDTH ** -0.5)
    lru_conv_b = nrm((nD, LRU_WIDTH), 0.02)
    lru_w_a = nrm((nD, LRU_BLOCKS, LRU_BLOCK, LRU_BLOCK), LRU_BLOCK ** -0.5)
    lru_b_a = nrm((nD, LRU_WIDTH), 0.02)
    lru_w_x = nrm((nD, LRU_BLOCKS, LRU_BLOCK, LRU_BLOCK), LRU_BLOCK ** -0.5)
    lru_b_x = nrm((nD, LRU_WIDTH), 0.02)
    p = unif((nD, LRU_WIDTH), LRU_A_MIN, LRU_A_MAX) ** (1.0 / LRU_C)
    lru_lambda = jnp.log(p) - jnp.log1p(-p)
    lru_w_out = nrm((nD, LRU_WIDTH, D), BETA * LRU_WIDTH ** -0.5)

    return {
        'x': x, 'ln_g': ln_g, 'ln_b': ln_b,
        'ffn_w1': ffn_w1, 'ffn_w3': ffn_w3, 'ffn_w2': ffn_w2,
        'ret_w_in': ret_w_in, 'ret_w_out': ret_w_out,
        's5_w_in': s5_w_in, 's5_a_re': s5_a_re, 's5_a_im': s5_a_im,
        's5_b_re': s5_b_re, 's5_b_im': s5_b_im, 's5_c_re': s5_c_re, 's5_c_im': s5_c_im,
        's5_d': s5_d, 's5_log_step': s5_log_step, 's5_w_glu': s5_w_glu, 's5_w_out': s5_w_out,
        'rw_mu': rw_mu, 'rw_w_r': rw_w_r, 'rw_w_k': rw_w_k, 'rw_w_v': rw_w_v,
        'rw_w0': rw_w0, 'rw_w1': rw_w1, 'rw_w2': rw_w2,
        'rw_a0': rw_a0, 'rw_a1': rw_a1, 'rw_a2': rw_a2,
        'rw_g1': rw_g1, 'rw_g2': rw_g2, 'rw_k_k': rw_k_k, 'rw_k_a': rw_k_a,
        'rw_r_k': rw_r_k, 'rw_lnx_g': rw_lnx_g, 'rw_lnx_b': rw_lnx_b, 'rw_w_o': rw_w_o,
        'lru_w_in': lru_w_in, 'lru_conv_w': lru_conv_w, 'lru_conv_b': lru_conv_b,
        'lru_w_a': lru_w_a, 'lru_b_a': lru_b_a, 'lru_w_x': lru_w_x, 'lru_b_x': lru_b_x,
        'lru_lambda': lru_lambda, 'lru_w_out': lru_w_out,
    }


def reference(x, ln_g, ln_b, ffn_w1, ffn_w3, ffn_w2, ret_w_in, ret_w_out,
              s5_w_in, s5_a_re, s5_a_im, s5_b_re, s5_b_im, s5_c_re, s5_c_im,
              s5_d, s5_log_step, s5_w_glu, s5_w_out,
              rw_mu, rw_w_r, rw_w_k, rw_w_v, rw_w0, rw_w1, rw_w2,
              rw_a0, rw_a1, rw_a2, rw_g1, rw_g2, rw_k_k, rw_k_a,
              rw_r_k, rw_lnx_g, rw_lnx_b, rw_w_o,
              lru_w_in, lru_conv_w, lru_conv_b, lru_w_a, lru_b_a, lru_w_x, lru_b_x,
              lru_lambda, lru_w_out):
    for i in range(DEPTH):
        m, j = i % N_MIXERS, i // N_MIXERS
        x = layer_norm(ALPHA * x + FFN_HALF * swiglu(x, ffn_w1[i, 0], ffn_w3[i, 0], ffn_w2[i, 0]),
                       ln_g[i, 0], ln_b[i, 0])
        if m == 0:
            y = retention(x, ret_w_in[j], ret_w_out[j])
        elif m == 1:
            y = s5_mixer(x, s5_w_in[j], s5_a_re[j], s5_a_im[j], s5_b_re[j], s5_b_im[j],
                         s5_c_re[j], s5_c_im[j], s5_d[j], s5_log_step[j], s5_w_glu[j], s5_w_out[j])
        elif m == 2:
            y = rwkv7_mixer(x, rw_mu[j], rw_w_r[j], rw_w_k[j], rw_w_v[j], rw_w0[j], rw_w1[j], rw_w2[j],
                            rw_a0[j], rw_a1[j], rw_a2[j], rw_g1[j], rw_g2[j], rw_k_k[j], rw_k_a[j],
                            rw_r_k[j], rw_lnx_g[j], rw_lnx_b[j], rw_w_o[j])
        else:
            y = rglru_block(x, lru_w_in[j], lru_conv_w[j], lru_conv_b[j], lru_w_a[j], lru_b_a[j],
                            lru_w_x[j], lru_b_x[j], lru_lambda[j], lru_w_out[j])
        x = layer_norm(ALPHA * x + y, ln_g[i, 1], ln_b[i, 1])
        x = layer_norm(ALPHA * x + FFN_HALF * swiglu(x, ffn_w1[i, 1], ffn_w3[i, 1], ffn_w2[i, 1]),
                       ln_g[i, 2], ln_b[i, 2])
    return x
```

```python
import functools
import math

import jax
import jax.numpy as jnp
from jax import lax
from jax.experimental import pallas as pl
from jax.experimental.pallas import tpu as pltpu

F32 = jnp.float32
BF16 = jnp.bfloat16

D_MODEL = 1024
DEPTH = 4
N_MIXERS = 4
ALPHA = (2 * DEPTH) ** 0.25
LN_EPS = 1e-5
D_FF = 2816
FFN_HALF = 0.5

RET_HEADS = 4
RET_DK = 256
RET_DV = 512
ROPE_BASE = 10000.0
RET_GN_EPS = 1e-5

S5_GROUPS = 64
S5_GROUP = 16
S5_STATE = 64
S5_SUB = 16
S5_GB = 8

RW_HEADS = 16
RW_HEAD = 64
RW_GN_EPS = 64e-5
RW_CHUNK = 64
RW_HB = 2

LRU_BLOCKS = 4
LRU_BLOCK = 256
CONV_WIDTH = 4
LRU_C = 8.0

VMEM_LIMIT = 56 * 1024 * 1024


def _cparams(*sem):
    return pltpu.CompilerParams(dimension_semantics=sem, vmem_limit_bytes=VMEM_LIMIT)


def _const_spec(shape):
    nd = len(shape)
    return pl.BlockSpec(shape, lambda *_: (0,) * nd, pipeline_mode=pl.Buffered(1))


def _layer_norm(r, g, b):
    mu = jnp.mean(r, axis=-1, keepdims=True)
    d = r - mu
    var = jnp.mean(d * d, axis=-1, keepdims=True)
    return d * lax.rsqrt(var + LN_EPS) * g + b


def _dot(a, b):
    return jnp.dot(a, b, preferred_element_type=F32)


def _dot_bf(a, b):
    return jnp.dot(a.astype(BF16), b.astype(BF16), preferred_element_type=F32)


def _softplus(z):
    return jnp.maximum(z, 0.0) + jnp.log(1.0 + jnp.exp(-jnp.abs(z)))


def _ffn_kernel(x_ref, w1_ref, w3_ref, w2_ref, g_ref, b_ref, o_ref):
    x = x_ref[...]
    xb = x.astype(BF16)
    h1 = _dot(xb, w1_ref[...])
    h3 = _dot(xb, w3_ref[...])
    act = (h1 * jax.nn.sigmoid(h1) * h3).astype(BF16)
    y = _dot(act, w2_ref[...])
    o_ref[...] = _layer_norm(ALPHA * x + FFN_HALF * y, g_ref[...], b_ref[...])


def _ffn_ln(x, w1, w3, w2, g, b, tm=512):
    m = x.shape[0]
    tm = min(tm, m)
    tok = pl.BlockSpec((tm, D_MODEL), lambda i: (i, 0))
    return pl.pallas_call(
        _ffn_kernel,
        grid=(m // tm,),
        in_specs=[tok, _const_spec(w1.shape), _const_spec(w3.shape), _const_spec(w2.shape),
                  _const_spec(g.shape), _const_spec(b.shape)],
        out_specs=tok,
        out_shape=jax.ShapeDtypeStruct((m, D_MODEL), F32),
        compiler_params=_cparams("parallel"),
        name="ffn_ln",
    )(x, w1, w3, w2, g, b)


def _rotate(t1, t2, cos, sin):
    return jnp.concatenate([t1 * cos - t2 * sin, t1 * sin + t2 * cos], axis=-1)


def _ret_kernel(x_ref, cos_ref, sin_ref, win_ref, wout_ref, dmat_ref, qd_ref, kd_ref,
                g_ref, b_ref, o_ref, state_ref):
    @pl.when(pl.program_id(1) == 0)
    def _():
        state_ref[...] = jnp.zeros_like(state_ref)

    H, DK, DV = RET_HEADS, RET_DK, RET_DV
    half = DK // 2
    x = x_ref[...]
    tc = x.shape[0]
    proj = _dot(x.astype(BF16), win_ref[...])
    cos = cos_ref[...]
    sin = sin_ref[...]
    y = jnp.zeros_like(x)
    for h in range(H):
        q0 = h * DK
        k0 = H * DK + h * DK
        v0 = 2 * H * DK + h * DV
        g0 = 2 * H * DK + H * DV + h * DV
        qh = _rotate(proj[:, q0:q0 + half], proj[:, q0 + half:q0 + DK], cos, sin)
        kh = _rotate(proj[:, k0:k0 + half], proj[:, k0 + half:k0 + DK], cos, sin) * (DK ** -0.5)
        vh = proj[:, v0:v0 + DV].astype(BF16)
        gate = proj[:, g0:g0 + DV]
        s = lax.dot_general(qh.astype(BF16), kh.astype(BF16), (((1,), (1,)), ((), ())),
                            preferred_element_type=F32) * dmat_ref[h]
        state = state_ref[h]
        o = _dot(s.astype(BF16), vh) + _dot_bf(qh * qd_ref[h], state)
        kdec = (kh * kd_ref[h]).T.astype(BF16)
        chunk_decay = math.exp(tc * math.log1p(-(2.0 ** (-5.0 - h))))
        state_ref[h] = state * chunk_decay + _dot(kdec, vh)
        mu = jnp.mean(o, axis=-1, keepdims=True)
        d = o - mu
        var = jnp.mean(d * d, axis=-1, keepdims=True)
        o = d * lax.rsqrt(var + RET_GN_EPS)
        o = gate * jax.nn.sigmoid(gate) * o
        y = y + _dot(o.astype(BF16), wout_ref[h * DV:(h + 1) * DV, :])
    o_ref[...] = _layer_norm(ALPHA * x + y, g_ref[...], b_ref[...])


def _retention_ln(x, batch, w_in, w_out, g, b, tc=256):
    m = x.shape[0]
    seq = m // batch
    tc = min(tc, seq)
    nt = seq // tc
    H, DK = RET_HEADS, RET_DK
    half = DK // 2
    inv = ROPE_BASE ** (-jnp.arange(half, dtype=F32) / half)
    ang = jnp.arange(seq, dtype=F32)[:, None] * inv[None, :]
    cos, sin = jnp.cos(ang), jnp.sin(ang)
    log_gamma = jnp.log1p(-jnp.power(2.0, -5.0 - jnp.arange(H, dtype=F32)))
    pos = jnp.arange(tc, dtype=F32)
    rel = pos[:, None] - pos[None, :]
    dmat = jnp.where(rel >= 0, jnp.exp(jnp.maximum(rel, 0.0)[None] * log_gamma[:, None, None]), 0.0)
    qd = jnp.broadcast_to(jnp.exp((pos + 1.0)[None, :, None] * log_gamma[:, None, None]), (H, tc, DK))
    kd = jnp.broadcast_to(jnp.exp((tc - 1.0 - pos)[None, :, None] * log_gamma[:, None, None]), (H, tc, DK))

    tok = pl.BlockSpec((tc, D_MODEL), lambda bi, ti: (bi * nt + ti, 0))
    rope = pl.BlockSpec((tc, half), lambda bi, ti: (ti, 0))
    return pl.pallas_call(
        _ret_kernel,
        grid=(batch, nt),
        in_specs=[tok, rope, rope, _const_spec(w_in.shape), _const_spec(w_out.shape),
                  _const_spec(dmat.shape), _const_spec(qd.shape), _const_spec(kd.shape),
                  _const_spec(g.shape), _const_spec(b.shape)],
        out_specs=tok,
        out_shape=jax.ShapeDtypeStruct((m, D_MODEL), F32),
        scratch_shapes=[pltpu.VMEM((H, DK, RET_DV), F32)],
        compiler_params=_cparams("parallel", "arbitrary"),
        name="retention_ln",
    )(x, cos, sin, w_in, w_out, dmat, qd, kd, g, b)


def _proj_kernel(x_ref, w_ref, o_ref):
    o_ref[...] = _dot(x_ref[...].astype(BF16), w_ref[...])


def _proj(x, w, tm=512):
    m = x.shape[0]
    tm = min(tm, m)
    n = w.shape[1]
    return pl.pallas_call(
        _proj_kernel,
        grid=(m // tm,),
        in_specs=[pl.BlockSpec((tm, D_MODEL), lambda i: (i, 0)), _const_spec(w.shape)],
        out_specs=pl.BlockSpec((tm, n), lambda i: (i, 0)),
        out_shape=jax.ShapeDtypeStruct((m, n), F32),
        compiler_params=_cparams("parallel"),
        name="proj",
    )(x, w)


def _s5_scan_kernel(u_ref, toep_ref, bc_ref, bcs_ref, cc_ref, a1_ref, a2_ref, a2s_ref, y_ref,
                    s_ref, ss_ref, h_ref, hs_ref):
    @pl.when(pl.program_id(2) == 0)
    def _():
        h_ref[...] = jnp.zeros_like(h_ref)
        hs_ref[...] = jnp.zeros_like(hs_ref)

    gb, rb, _ = u_ref.shape
    w = 2 * S5_STATE
    for gl in range(gb):
        ub = u_ref[gl].astype(BF16)
        y_ref[gl] = _dot(ub, toep_ref[gl])
        s_ref[:, gl * w:(gl + 1) * w] = _dot(ub, bc_ref[gl])
        ss_ref[:, gl * w:(gl + 1) * w] = _dot(ub, bcs_ref[gl])

    a1 = a1_ref[...]
    a2 = a2_ref[...]
    a2s = a2s_ref[...]

    def row(c, carry):
        h, hs = carry
        s = s_ref[pl.ds(c, 1), :]
        ssw = ss_ref[pl.ds(c, 1), :]
        s_ref[pl.ds(c, 1), :] = h
        return a1 * h + a2 * hs + s, a1 * hs + a2s * h + ssw

    h, hs = lax.fori_loop(0, rb, row, (h_ref[...], hs_ref[...]))
    h_ref[...] = h
    hs_ref[...] = hs
    for gl in range(gb):
        y_ref[gl] = y_ref[gl] + _dot_bf(s_ref[:, gl * w:(gl + 1) * w], cc_ref[gl])


def _s5_operators(a_re, a_im, b_re, b_im, c_re, c_im, log_step):
    L = S5_SUB
    G, N, P = b_re.shape
    dt = jnp.exp(log_step)[:, None]
    mag = jnp.exp(dt * a_re)
    abar_re = mag * jnp.cos(dt * a_im)
    abar_im = mag * jnp.sin(dt * a_im)
    den = a_re * a_re + a_im * a_im
    f_re = ((abar_re - 1.0) * a_re + abar_im * a_im) / den
    f_im = (abar_im * a_re - (abar_re - 1.0) * a_im) / den
    bb_re = f_re[..., None] * b_re - f_im[..., None] * b_im
    bb_im = f_re[..., None] * b_im + f_im[..., None] * b_re
    pr, pi = [jnp.ones_like(abar_re)], [jnp.zeros_like(abar_re)]
    for _ in range(L):
        pr.append(pr[-1] * abar_re - pi[-1] * abar_im)
        pi.append(pr[-2] * abar_im + pi[-1] * abar_re)
    pw_re, pw_im = jnp.stack(pr, 0), jnp.stack(pi, 0)
    cp_re = c_re[None] * pw_re[:, :, None, :] - c_im[None] * pw_im[:, :, None, :]
    cp_im = c_re[None] * pw_im[:, :, None, :] + c_im[None] * pw_re[:, :, None, :]
    kern = (jnp.einsum('lgpn,gnq->lgpq', cp_re, bb_re, precision=lax.Precision.HIGHEST)
            - jnp.einsum('lgpn,gnq->lgpq', cp_im, bb_im, precision=lax.Precision.HIGHEST))
    i = jnp.arange(L)
    lag = i[None, :] - i[:, None]
    kl = jnp.where((lag >= 0)[:, :, None, None, None], kern[jnp.clip(lag, 0, L)], 0.0)
    toep = kl.transpose(2, 0, 4, 1, 3).reshape(G, L * P, L * P)
    pwr, pwi = pw_re[L - 1 - i], pw_im[L - 1 - i]
    bc_re = pwr[..., None] * bb_re[None] - pwi[..., None] * bb_im[None]
    bc_im = pwr[..., None] * bb_im[None] + pwi[..., None] * bb_re[None]
    to_rows = lambda t: t.transpose(1, 0, 3, 2).reshape(G, L * P, 2 * N)
    bc = to_rows(jnp.concatenate([bc_re, bc_im], axis=2))
    bcs = to_rows(jnp.concatenate([bc_im, bc_re], axis=2))
    cc = jnp.concatenate([cp_re[1:L + 1], -cp_im[1:L + 1]], axis=-1)
    cc = cc.transpose(1, 3, 0, 2).reshape(G, 2 * N, L * P)
    al_re, al_im = pw_re[L], pw_im[L]
    a1 = jnp.concatenate([al_re, al_re], -1).reshape(1, G * 2 * N)
    a2 = jnp.concatenate([-al_im, al_im], -1).reshape(1, G * 2 * N)
    a2s = jnp.concatenate([al_im, -al_im], -1).reshape(1, G * 2 * N)
    return toep.astype(BF16), bc.astype(BF16), bcs.astype(BF16), cc.astype(BF16), a1, a2, a2s


def _s5_out_kernel(x_ref, y_ref, u_ref, d_ref, wglu_ref, wout_ref, g_ref, b_ref, o_ref):
    yv = y_ref[...] + d_ref[...] * u_ref[...]
    act = jax.nn.gelu(yv)
    z = act * jax.nn.sigmoid(_dot(act.astype(BF16), wglu_ref[...]))
    out = _dot(z.astype(BF16), wout_ref[...])
    o_ref[...] = _layer_norm(ALPHA * x_ref[...] + out, g_ref[...], b_ref[...])


def _s5_ln(x, batch, w_in, ops, d_skip, w_glu, w_out, g, b, tm=512):
    m = x.shape[0]
    seq = m // batch
    G, P, L, N = S5_GROUPS, S5_GROUP, S5_SUB, S5_STATE
    toep, bc, bcs, cc, a1, a2, a2s = ops
    u = _proj(x, w_in)
    nr = seq // L
    ug = u.reshape(batch, nr, L, G, P).transpose(3, 0, 1, 2, 4).reshape(G, batch * nr, L * P)
    rb = min(512, nr)
    nrb = nr // rb
    gb = S5_GB
    w = 2 * N
    blk = pl.BlockSpec((gb, rb, L * P), lambda gi, bi, ri: (gi, bi * nrb + ri, 0))
    wspec = lambda shp: pl.BlockSpec((gb,) + shp, lambda gi, bi, ri: (gi, 0, 0))
    aspec = pl.BlockSpec((1, gb * w), lambda gi, bi, ri: (0, gi))
    yg = pl.pallas_call(
        _s5_scan_kernel,
        grid=(G // gb, batch, nrb),
        in_specs=[blk, wspec((L * P, L * P)), wspec((L * P, w)), wspec((L * P, w)), wspec((w, L * P)),
                  aspec, aspec, aspec],
        out_specs=blk,
        out_shape=jax.ShapeDtypeStruct((G, batch * nr, L * P), F32),
        scratch_shapes=[pltpu.VMEM((rb, gb * w), F32), pltpu.VMEM((rb, gb * w), F32),
                        pltpu.VMEM((1, gb * w), F32), pltpu.VMEM((1, gb * w), F32)],
        compiler_params=_cparams("parallel", "arbitrary", "arbitrary"),
        name="s5_scan",
    )(ug, toep, bc, bcs, cc, a1, a2, a2s)
    y = yg.reshape(G, batch, nr, L, P).transpose(1, 2, 3, 0, 4).reshape(m, G * P)

    tm = min(tm, m)
    tok = pl.BlockSpec((tm, D_MODEL), lambda i: (i, 0))
    return pl.pallas_call(
        _s5_out_kernel,
        grid=(m // tm,),
        in_specs=[tok, tok, tok, _const_spec(d_skip.shape), _const_spec(w_glu.shape),
                  _const_spec(w_out.shape), _const_spec(g.shape), _const_spec(b.shape)],
        out_specs=tok,
        out_shape=jax.ShapeDtypeStruct((m, D_MODEL), F32),
        compiler_params=_cparams("parallel"),
        name="s5_out_ln",
    )(x, y, u, d_skip, w_glu, w_out, g, b)


def _rw_proj_kernel(x_ref, xp_ref, mu_ref, wr_ref, wk_ref, wv_ref, w0_ref, w1_ref, w2_ref,
                    a0_ref, a1_ref, a2_ref, g1_ref, g2_ref,
                    r_ref, k_ref, v_ref, lw_ref, a_ref, g_ref, *, tiles_per_seq):
    x = x_ref[...]
    first = (pl.program_id(0) % tiles_per_seq) == 0
    prev = jnp.where(first, 0.0, xp_ref[7:8, :])
    rows = lax.broadcasted_iota(jnp.int32, x.shape, 0)
    shifted = jnp.where(rows == 0, prev, pltpu.roll(x, 1, 0))
    xx = shifted - x
    mix = lambda i: (x + xx * mu_ref[i:i + 1, :]).astype(BF16)
    r_ref[...] = _dot(mix(0), wr_ref[...])
    z = w0_ref[...] + _dot_bf(jnp.tanh(_dot(mix(1), w1_ref[...])), w2_ref[...])
    w = -_softplus(-z) - 0.5
    lw_ref[...] = -jnp.exp(w)
    k_ref[...] = _dot(mix(2), wk_ref[...])
    v_ref[...] = _dot(mix(3), wv_ref[...])
    a_ref[...] = jax.nn.sigmoid(a0_ref[...] + _dot_bf(_dot(mix(4), a1_ref[...]), a2_ref[...]))
    g_ref[...] = _dot_bf(jax.nn.sigmoid(_dot(mix(5), g1_ref[...])), g2_ref[...])


def _rw_scan_kernel(r_ref, k_ref, v_ref, lw_ref, a_ref, kk_ref, ka_ref, rk_ref, lg_ref, lb_ref,
                    mask_ref, y_ref, state_ref):
    @pl.when(pl.program_id(2) == 0)
    def _():
        state_ref[...] = jnp.zeros_like(state_ref)

    L, N = RW_CHUNK, RW_HEAD
    tm = r_ref.shape[0]
    strict = mask_ref[0]
    incl = mask_ref[1]

    def chunk(c, carry):
        rows = pl.ds(pl.multiple_of(c * L, L), L)
        for hl in range(RW_HB):
            lanes = slice(hl * N, (hl + 1) * N)
            r = r_ref[rows, lanes]
            k = k_ref[rows, lanes]
            v = v_ref[rows, lanes]
            lw = lw_ref[rows, lanes]
            a = a_ref[rows, lanes]
            kk = k * kk_ref[:, lanes]
            kk = kk / jnp.maximum(jnp.sqrt(jnp.sum(kk * kk, axis=-1, keepdims=True)), 1e-12)
            k = k * (1.0 + (a - 1.0) * ka_ref[:, lanes])
            beta = kk * a
            hi = lw.astype(BF16)
            r1 = lw - hi.astype(F32)
            mid = r1.astype(BF16)
            lo = (r1 - mid.astype(F32)).astype(BF16)
            tri = incl.astype(BF16)
            cum = _dot(tri, hi) + _dot(tri, mid) + _dot(tri, lo)
            g_in = jnp.exp(cum)
            g_ex = jnp.exp(cum - lw)
            g_inv = jnp.exp(-cum)
            rt = (r * g_in).astype(BF16)
            at = (-kk * g_ex).astype(BF16)
            kt = k * g_inv
            bt = beta * g_inv
            nt = (((1,), (1,)), ((), ()))
            ktb = kt.astype(BF16)
            btb = bt.astype(BF16)
            a_ab = lax.dot_general(at, btb, nt, preferred_element_type=F32) * strict
            a_ak = lax.dot_general(at, ktb, nt, preferred_element_type=F32) * strict
            b_rb = lax.dot_general(rt, btb, nt, preferred_element_type=F32) * incl
            b_rk = lax.dot_general(rt, ktb, nt, preferred_element_type=F32) * incl
            rr = lax.broadcasted_iota(jnp.int32, (L, L), 0)
            cc = lax.broadcasted_iota(jnp.int32, (L, L), 1)
            tinv = jnp.where(rr == cc, 1.0, 0.0).astype(F32)
            lvl = 0
            bsz = 1
            while bsz < L:
                off = a_ab * mask_ref[2 + lvl]
                tinv = tinv + _dot_bf(tinv, _dot_bf(off, tinv))
                bsz *= 2
                lvl += 1
            state = state_ref[hl]
            sb = state.astype(BF16)
            vb = v.astype(BF16)
            tn = (((0,), (0,)), ((), ()))
            wmat = lax.dot_general(at, sb, nt, preferred_element_type=F32) + _dot_bf(a_ak, vb)
            u = _dot_bf(tinv, wmat)
            ub = u.astype(BF16)
            o = (lax.dot_general(rt, sb, nt, preferred_element_type=F32)
                 + _dot_bf(b_rb, ub) + _dot_bf(b_rk, vb))
            g_last = g_in[L - 1:L, :]
            bl = (bt * g_last).astype(BF16)
            kl = (kt * g_last).astype(BF16)
            state_ref[hl] = (state * g_last + lax.dot_general(ub, bl, tn, preferred_element_type=F32)
                             + lax.dot_general(vb, kl, tn, preferred_element_type=F32))
            mu = jnp.mean(o, axis=-1, keepdims=True)
            d = o - mu
            var = jnp.mean(d * d, axis=-1, keepdims=True)
            o = d * lax.rsqrt(var + RW_GN_EPS) * lg_ref[:, lanes] + lb_ref[:, lanes]
            o = o + jnp.sum(r * k * rk_ref[:, lanes], axis=-1, keepdims=True) * v
            y_ref[rows, lanes] = o
        return carry

    lax.fori_loop(0, tm // L, chunk, 0)


def _rw_out_kernel(x_ref, y_ref, gate_ref, wo_ref, g_ref, b_ref, o_ref):
    out = _dot((y_ref[...] * gate_ref[...]).astype(BF16), wo_ref[...])
    o_ref[...] = _layer_norm(ALPHA * x_ref[...] + out, g_ref[...], b_ref[...])


def _rw_masks():
    L = RW_CHUNK
    i = jnp.arange(L)
    r, c = i[:, None], i[None, :]
    masks = [r > c, r >= c]
    bsz = 1
    while bsz < L:
        masks.append((r // (2 * bsz) == c // (2 * bsz)) & ((r // bsz) % 2 == 1) & ((c // bsz) % 2 == 0))
        bsz *= 2
    return jnp.stack(masks, 0).astype(F32)


def _rwkv_ln(x, batch, p, g, b, tm=512):
    m = x.shape[0]
    seq = m // batch
    tm = min(tm, seq)
    nt = seq // tm
    tok = pl.BlockSpec((tm, D_MODEL), lambda i: (i, 0))
    prev = pl.BlockSpec((8, D_MODEL), lambda i: (jnp.maximum(i * (tm // 8) - 1, 0), 0))
    names = ['mu', 'w_r', 'w_k', 'w_v', 'w0', 'w1', 'w2', 'a0', 'a1', 'a2', 'g1', 'g2']
    ws = [p[n] for n in names]
    act = jax.ShapeDtypeStruct((m, D_MODEL), F32)
    r, k, v, lw, a, gate = pl.pallas_call(
        functools.partial(_rw_proj_kernel, tiles_per_seq=nt),
        grid=(m // tm,),
        in_specs=[tok, prev] + [_const_spec(w.shape) for w in ws],
        out_specs=[tok] * 6,
        out_shape=[act] * 6,
        compiler_params=_cparams("parallel"),
        name="rwkv_proj",
    )(x, x, *ws)

    lanes = RW_HB * RW_HEAD
    hblk = pl.BlockSpec((tm, lanes), lambda bi, hi, ti: (bi * nt + ti, hi))
    pblk = pl.BlockSpec((1, lanes), lambda bi, hi, ti: (0, hi))
    masks = _rw_masks()
    y = pl.pallas_call(
        _rw_scan_kernel,
        grid=(batch, RW_HEADS // RW_HB, nt),
        in_specs=[hblk] * 5 + [pblk] * 5 + [_const_spec(masks.shape)],
        out_specs=hblk,
        out_shape=act,
        scratch_shapes=[pltpu.VMEM((RW_HB, RW_HEAD, RW_HEAD), F32)],
        compiler_params=_cparams("parallel", "parallel", "arbitrary"),
        name="rwkv_scan",
    )(r, k, v, lw, a, p['k_k'], p['k_a'], p['r_k'], p['lnx_g'], p['lnx_b'], masks)

    return pl.pallas_call(
        _rw_out_kernel,
        grid=(m // tm,),
        in_specs=[tok, tok, tok, _const_spec(p['w_o'].shape), _const_spec(g.shape), _const_spec(b.shape)],
        out_specs=tok,
        out_shape=act,
        compiler_params=_cparams("parallel"),
        name="rwkv_out_ln",
    )(x, y, gate, p['w_o'], g, b)


def _lru_kernel(x_ref, win_ref, cw_ref, cb_ref, wa_ref, ba_ref, wx_ref, bx_ref, lam_ref, wout_ref,
                g_ref, b_ref, o_ref, xr_ref, a_ref, inp_ref, h_ref):
    W = D_MODEL
    tm = x_ref.shape[0]

    @pl.when(pl.program_id(1) == 0)
    def _():
        xr_ref[...] = jnp.zeros_like(xr_ref)
        h_ref[...] = jnp.zeros_like(h_ref)

    x = x_ref[...]
    proj = _dot(x.astype(BF16), win_ref[...])
    gate = jax.nn.gelu(proj[:, :W])
    xr_ref[0:8, :] = xr_ref[tm:tm + 8, :]
    xr_ref[8:tm + 8, :] = proj[:, W:]
    xc = cb_ref[...] + cw_ref[3:4, :] * xr_ref[8:tm + 8, :]
    for j in range(CONV_WIDTH - 1):
        xc = xc + cw_ref[j:j + 1, :] * xr_ref[pl.ds(5 + j, tm), :]
    xcb = xc.astype(BF16)
    gr = jnp.concatenate([_dot(xcb[:, kb * LRU_BLOCK:(kb + 1) * LRU_BLOCK], wa_ref[kb])
                          for kb in range(LRU_BLOCKS)], axis=-1) + ba_ref[...]
    gi = jnp.concatenate([_dot(xcb[:, kb * LRU_BLOCK:(kb + 1) * LRU_BLOCK], wx_ref[kb])
                          for kb in range(LRU_BLOCKS)], axis=-1) + bx_ref[...]
    log_a = -LRU_C * jax.nn.sigmoid(gr) * _softplus(-lam_ref[...])
    av = jnp.exp(log_a)
    a_ref[...] = av
    inp_ref[...] = jnp.sqrt(1.0 - av * av) * (jax.nn.sigmoid(gi) * xc)

    def block(i, h):
        rows = pl.ds(pl.multiple_of(i * 8, 8), 8)
        ab = a_ref[rows, :]
        bb = inp_ref[rows, :]
        out = []
        for j in range(8):
            h = ab[j:j + 1, :] * h + bb[j:j + 1, :]
            out.append(h)
        inp_ref[rows, :] = jnp.concatenate(out, axis=0)
        return h

    h_ref[...] = lax.fori_loop(0, tm // 8, block, h_ref[...])
    y = _dot((inp_ref[...] * gate).astype(BF16), wout_ref[...])
    o_ref[...] = _layer_norm(ALPHA * x + y, g_ref[...], b_ref[...])


def _lru_ln(x, batch, p, g, b, tm=256):
    m = x.shape[0]
    seq = m // batch
    tm = min(tm, seq)
    nt = seq // tm
    tok = pl.BlockSpec((tm, D_MODEL), lambda bi, ti: (bi * nt + ti, 0))
    names = ['w_in', 'conv_w', 'conv_b', 'w_a', 'b_a', 'w_x', 'b_x', 'lam', 'w_out']
    ws = [p[n] for n in names]
    return pl.pallas_call(
        _lru_kernel,
        grid=(batch, nt),
        in_specs=[tok] + [_const_spec(w.shape) for w in ws] + [_const_spec(g.shape), _const_spec(b.shape)],
        out_specs=tok,
        out_shape=jax.ShapeDtypeStruct((m, D_MODEL), F32),
        scratch_shapes=[pltpu.VMEM((tm + 8, D_MODEL), F32), pltpu.VMEM((tm, D_MODEL), F32),
                        pltpu.VMEM((tm, D_MODEL), F32), pltpu.VMEM((1, D_MODEL), F32)],
        compiler_params=_cparams("parallel", "arbitrary"),
        name="rglru_ln",
    )(x, *ws, g, b)


def kernel(x, ln_g, ln_b, ffn_w1, ffn_w3, ffn_w2, ret_w_in, ret_w_out, s5_w_in, s5_a_re, s5_a_im, s5_b_re, s5_b_im, s5_c_re, s5_c_im, s5_d, s5_log_step, s5_w_glu, s5_w_out, rw_mu, rw_w_r, rw_w_k, rw_w_v, rw_w0, rw_w1, rw_w2, rw_a0, rw_a1, rw_a2, rw_g1, rw_g2, rw_k_k, rw_k_a, rw_r_k, rw_lnx_g, rw_lnx_b, rw_w_o, lru_w_in, lru_conv_w, lru_conv_b, lru_w_a, lru_b_a, lru_w_x, lru_b_x, lru_lambda, lru_w_out):
    batch, seq, d = x.shape
    depth = ln_g.shape[0]
    h = x.reshape(batch * seq, d)
    bf = lambda t: t.astype(BF16)
    row = lambda t: t.reshape(1, -1)
    for i in range(depth):
        m, j = i % N_MIXERS, i // N_MIXERS
        h = _ffn_ln(h, bf(ffn_w1[i, 0]), bf(ffn_w3[i, 0]), bf(ffn_w2[i, 0]), row(ln_g[i, 0]), row(ln_b[i, 0]))
        g, b = row(ln_g[i, 1]), row(ln_b[i, 1])
        if m == 0:
            h = _retention_ln(h, batch, bf(ret_w_in[j]), bf(ret_w_out[j]), g, b)
        elif m == 1:
            ops = _s5_operators(s5_a_re[j], s5_a_im[j], s5_b_re[j], s5_b_im[j], s5_c_re[j], s5_c_im[j],
                                s5_log_step[j])
            h = _s5_ln(h, batch, bf(s5_w_in[j]), ops, row(s5_d[j]), bf(s5_w_glu[j]), bf(s5_w_out[j]), g, b)
        elif m == 2:
            p = dict(mu=rw_mu[j], w_r=bf(rw_w_r[j]), w_k=bf(rw_w_k[j]), w_v=bf(rw_w_v[j]),
                     w0=row(rw_w0[j]), w1=bf(rw_w1[j]), w2=bf(rw_w2[j]),
                     a0=row(rw_a0[j]), a1=bf(rw_a1[j]), a2=bf(rw_a2[j]), g1=bf(rw_g1[j]), g2=bf(rw_g2[j]),
                     k_k=row(rw_k_k[j]), k_a=row(rw_k_a[j]), r_k=row(rw_r_k[j]),
                     lnx_g=row(rw_lnx_g[j]), lnx_b=row(rw_lnx_b[j]), w_o=bf(rw_w_o[j]))
            h = _rwkv_ln(h, batch, p, g, b)
        else:
            p = dict(w_in=bf(lru_w_in[j]), conv_w=lru_conv_w[j], conv_b=row(lru_conv_b[j]),
                     w_a=bf(lru_w_a[j]), b_a=row(lru_b_a[j]), w_x=bf(lru_w_x[j]), b_x=row(lru_b_x[j]),
                     lam=row(lru_lambda[j]), w_out=bf(lru_w_out[j]))
            h = _lru_ln(h, batch, p, g, b)
        h = _ffn_ln(h, bf(ffn_w1[i, 1]), bf(ffn_w3[i, 1]), bf(ffn_w2[i, 1]), row(ln_g[i, 2]), row(ln_b[i, 2]))
    return h.reshape(batch, seq, d)
```

```python
import functools
import math

import jax
import jax.numpy as jnp
from jax import lax
from jax.experimental import pallas as pl
from jax.experimental.pallas import tpu as pltpu

F32 = jnp.float32
BF16 = jnp.bfloat16

D_MODEL = 1024
DEPTH = 4
N_MIXERS = 4
ALPHA = (2 * DEPTH) ** 0.25
LN_EPS = 1e-5
D_FF = 2816
FFN_HALF = 0.5

RET_HEADS = 4
RET_DK = 256
RET_DV = 512
ROPE_BASE = 10000.0
RET_GN_EPS = 1e-5

S5_GROUPS = 64
S5_GROUP = 16
S5_STATE = 64
S5_SUB = 16
S5_GB = 16
S5_NGB = S5_GROUPS // S5_GB

RW_HEADS = 16
RW_HEAD = 64
RW_GN_EPS = 64e-5
RW_CHUNK = 64
RW_HB = 4
RW_SCAN_TILE = 256

LRU_BLOCKS = 4
LRU_BLOCK = 256
CONV_WIDTH = 4
LRU_C = 8.0

VMEM_LIMIT = 56 * 1024 * 1024


def _cparams(*sem):
    return pltpu.CompilerParams(dimension_semantics=sem, vmem_limit_bytes=VMEM_LIMIT)


def _const_spec(shape):
    nd = len(shape)
    return pl.BlockSpec(shape, lambda *_: (0,) * nd, pipeline_mode=pl.Buffered(1))


def _layer_norm(r, g, b):
    mu = jnp.mean(r, axis=-1, keepdims=True)
    d = r - mu
    var = jnp.mean(d * d, axis=-1, keepdims=True)
    return d * lax.rsqrt(var + LN_EPS) * g + b


def _dot(a, b):
    return jnp.dot(a, b, preferred_element_type=F32)


def _dot_bf(a, b):
    return jnp.dot(a.astype(BF16), b.astype(BF16), preferred_element_type=F32)


def _softplus(z):
    return jnp.maximum(z, 0.0) + jnp.log(1.0 + jnp.exp(-jnp.abs(z)))


def _ffn_kernel(x_ref, w1_ref, w3_ref, w2_ref, g_ref, b_ref, o_ref):
    x = x_ref[...]
    xb = x.astype(BF16)
    h1 = _dot(xb, w1_ref[...])
    h3 = _dot(xb, w3_ref[...])
    act = (h1 * jax.nn.sigmoid(h1) * h3).astype(BF16)
    y = _dot(act, w2_ref[...])
    o_ref[...] = _layer_norm(ALPHA * x + FFN_HALF * y, g_ref[...], b_ref[...])


def _ffn_ln(x, w1, w3, w2, g, b, tm=512):
    m = x.shape[0]
    tm = min(tm, m)
    tok = pl.BlockSpec((tm, D_MODEL), lambda i: (i, 0))
    return pl.pallas_call(
        _ffn_kernel,
        grid=(m // tm,),
        in_specs=[tok, _const_spec(w1.shape), _const_spec(w3.shape), _const_spec(w2.shape),
                  _const_spec(g.shape), _const_spec(b.shape)],
        out_specs=tok,
        out_shape=jax.ShapeDtypeStruct((m, D_MODEL), F32),
        compiler_params=_cparams("parallel"),
        name="ffn_ln",
    )(x, w1, w3, w2, g, b)


def _rotate(t1, t2, cos, sin):
    return jnp.concatenate([t1 * cos - t2 * sin, t1 * sin + t2 * cos], axis=-1)


def _ret_kernel(x_ref, cos_ref, sin_ref, win_ref, wout_ref, dmat_ref, qd_ref, kd_ref,
                g_ref, b_ref, o_ref, state_ref):
    @pl.when(pl.program_id(1) == 0)
    def _():
        state_ref[...] = jnp.zeros_like(state_ref)

    H, DK, DV = RET_HEADS, RET_DK, RET_DV
    half = DK // 2
    x = x_ref[...]
    tc = x.shape[0]
    proj = _dot(x.astype(BF16), win_ref[...])
    cos = cos_ref[...]
    sin = sin_ref[...]
    y = jnp.zeros_like(x)
    for h in range(H):
        q0 = h * DK
        k0 = H * DK + h * DK
        v0 = 2 * H * DK + h * DV
        g0 = 2 * H * DK + H * DV + h * DV
        qh = _rotate(proj[:, q0:q0 + half], proj[:, q0 + half:q0 + DK], cos, sin)
        kh = _rotate(proj[:, k0:k0 + half], proj[:, k0 + half:k0 + DK], cos, sin) * (DK ** -0.5)
        vh = proj[:, v0:v0 + DV].astype(BF16)
        gate = proj[:, g0:g0 + DV]
        s = lax.dot_general(qh.astype(BF16), kh.astype(BF16), (((1,), (1,)), ((), ())),
                            preferred_element_type=F32) * dmat_ref[h]
        state = state_ref[h]
        o = _dot(s.astype(BF16), vh) + _dot_bf(qh * qd_ref[h], state)
        kdec = (kh * kd_ref[h]).T.astype(BF16)
        chunk_decay = math.exp(tc * math.log1p(-(2.0 ** (-5.0 - h))))
        state_ref[h] = state * chunk_decay + _dot(kdec, vh)
        mu = jnp.mean(o, axis=-1, keepdims=True)
        d = o - mu
        var = jnp.mean(d * d, axis=-1, keepdims=True)
        o = d * lax.rsqrt(var + RET_GN_EPS)
        o = gate * jax.nn.sigmoid(gate) * o
        y = y + _dot(o.astype(BF16), wout_ref[h * DV:(h + 1) * DV, :])
    o_ref[...] = _layer_norm(ALPHA * x + y, g_ref[...], b_ref[...])


def _retention_ln(x, batch, w_in, w_out, g, b, tc=256):
    m = x.shape[0]
    seq = m // batch
    tc = min(tc, seq)
    nt = seq // tc
    H, DK = RET_HEADS, RET_DK
    half = DK // 2
    inv = ROPE_BASE ** (-jnp.arange(half, dtype=F32) / half)
    ang = jnp.arange(seq, dtype=F32)[:, None] * inv[None, :]
    cos, sin = jnp.cos(ang), jnp.sin(ang)
    log_gamma = jnp.log1p(-jnp.power(2.0, -5.0 - jnp.arange(H, dtype=F32)))
    pos = jnp.arange(tc, dtype=F32)
    rel = pos[:, None] - pos[None, :]
    dmat = jnp.where(rel >= 0, jnp.exp(jnp.maximum(rel, 0.0)[None] * log_gamma[:, None, None]), 0.0)
    qd = jnp.broadcast_to(jnp.exp((pos + 1.0)[None, :, None] * log_gamma[:, None, None]), (H, tc, DK))
    kd = jnp.broadcast_to(jnp.exp((tc - 1.0 - pos)[None, :, None] * log_gamma[:, None, None]), (H, tc, DK))

    tok = pl.BlockSpec((tc, D_MODEL), lambda bi, ti: (bi * nt + ti, 0))
    rope = pl.BlockSpec((tc, half), lambda bi, ti: (ti, 0))
    return pl.pallas_call(
        _ret_kernel,
        grid=(batch, nt),
        in_specs=[tok, rope, rope, _const_spec(w_in.shape), _const_spec(w_out.shape),
                  _const_spec(dmat.shape), _const_spec(qd.shape), _const_spec(kd.shape),
                  _const_spec(g.shape), _const_spec(b.shape)],
        out_specs=tok,
        out_shape=jax.ShapeDtypeStruct((m, D_MODEL), F32),
        scratch_shapes=[pltpu.VMEM((H, DK, RET_DV), F32)],
        compiler_params=_cparams("parallel", "arbitrary"),
        name="retention_ln",
    )(x, cos, sin, w_in, w_out, dmat, qd, kd, g, b)


def _s5_kernel(x_ref, win_ref, bre_ref, bim_ref, cre_ref, cim_ref, pwr_ref, pwi_ref, d_ref,
               wglu_ref, wout_ref, g_ref, b_ref, o_ref, hre_ref, him_ref, ere_ref, eim_ref,
               sre_ref, sim_ref, y_ref):
    @pl.when(pl.program_id(1) == 0)
    def _():
        sre_ref[...] = jnp.zeros_like(sre_ref)
        sim_ref[...] = jnp.zeros_like(sim_ref)

    L = S5_SUB
    tm = x_ref.shape[0]
    nr = tm // L
    cols = S5_GB * S5_GROUP
    nlb = hre_ref.shape[0]
    x = x_ref[...]
    u = _dot(x.astype(BF16), win_ref[...])
    ub = u.astype(BF16)

    def put(ref, val):
        for kb in range(nlb):
            ref[kb] = val[:, kb * 128:(kb + 1) * 128]

    def get(ref):
        return jnp.concatenate([ref[kb] for kb in range(nlb)], axis=-1)

    for gb in range(S5_NGB):
        ug = ub[:, gb * cols:(gb + 1) * cols]
        put(hre_ref, _dot(ug, bre_ref[gb]))
        put(him_ref, _dot(ug, bim_ref[gb]))
        sub = lambda j: pl.ds(j, nr, stride=L)
        ar = pwr_ref[gb, 1]
        ai = pwi_ref[gb, 1]
        hr = hre_ref[:, sub(0), :]
        hi = him_ref[:, sub(0), :]
        for j in range(1, L):
            hr, hi = (ar * hr - ai * hi + hre_ref[:, sub(j), :], ar * hi + ai * hr + him_ref[:, sub(j), :])
            hre_ref[:, sub(j), :] = hr
            him_ref[:, sub(j), :] = hi
        ere_ref[...] = hr
        eim_ref[...] = hi
        alr = pwr_ref[gb, L]
        ali = pwi_ref[gb, L]

        def row(c, carry):
            sr, si = carry
            er = ere_ref[:, pl.ds(c, 1), :]
            ei = eim_ref[:, pl.ds(c, 1), :]
            ere_ref[:, pl.ds(c, 1), :] = sr
            eim_ref[:, pl.ds(c, 1), :] = si
            return alr * sr - ali * si + er, alr * si + ali * sr + ei

        sr, si = lax.fori_loop(0, nr, row, (sre_ref[gb], sim_ref[gb]))
        sre_ref[gb] = sr
        sim_ref[gb] = si
        pr_in = ere_ref[...]
        pi_in = eim_ref[...]
        for j in range(L):
            pr = pwr_ref[gb, j + 1]
            pi = pwi_ref[gb, j + 1]
            hre_ref[:, sub(j), :] = hre_ref[:, sub(j), :] + (pr * pr_in - pi * pi_in)
            him_ref[:, sub(j), :] = him_ref[:, sub(j), :] + (pr * pi_in + pi * pr_in)
        y_ref[:, gb * cols:(gb + 1) * cols] = (_dot_bf(get(hre_ref), cre_ref[gb])
                                               + _dot_bf(get(him_ref), cim_ref[gb]))
    yv = y_ref[...] + d_ref[...] * u
    act = jax.nn.gelu(yv)
    z = act * jax.nn.sigmoid(_dot(act.astype(BF16), wglu_ref[...]))
    out = _dot(z.astype(BF16), wout_ref[...])
    o_ref[...] = _layer_norm(ALPHA * x + out, g_ref[...], b_ref[...])


def _s5_operators(a_re, a_im, b_re, b_im, c_re, c_im, log_step):
    L = S5_SUB
    G, N, P = b_re.shape
    dt = jnp.exp(log_step)[:, None]
    mag = jnp.exp(dt * a_re)
    abar_re = mag * jnp.cos(dt * a_im)
    abar_im = mag * jnp.sin(dt * a_im)
    den = a_re * a_re + a_im * a_im
    f_re = ((abar_re - 1.0) * a_re + abar_im * a_im) / den
    f_im = (abar_im * a_re - (abar_re - 1.0) * a_im) / den
    bb_re = f_re[..., None] * b_re - f_im[..., None] * b_im
    bb_im = f_re[..., None] * b_im + f_im[..., None] * b_re
    pr, pi = [jnp.ones_like(abar_re)], [jnp.zeros_like(abar_re)]
    for _ in range(L):
        pr.append(pr[-1] * abar_re - pi[-1] * abar_im)
        pi.append(pr[-2] * abar_im + pi[-1] * abar_re)
    gbn, ngb = S5_GB, S5_NGB
    eye = jnp.eye(gbn, dtype=F32)
    blockdiag_in = lambda t: jnp.einsum('bgnp,gh->bgphn', t.reshape(ngb, gbn, N, P), eye).reshape(
        ngb, gbn * P, gbn * N)
    blockdiag_out = lambda t: jnp.einsum('bgpn,gh->bgnhp', t.reshape(ngb, gbn, P, N), eye).reshape(
        ngb, gbn * N, gbn * P)
    planes = lambda t: jnp.stack(t, 0).reshape(L + 1, ngb, gbn * N // 128, 1, 128).transpose(1, 0, 2, 3, 4)
    return (blockdiag_in(bb_re).astype(BF16), blockdiag_in(bb_im).astype(BF16),
            blockdiag_out(c_re).astype(BF16), blockdiag_out(-c_im).astype(BF16), planes(pr), planes(pi))


def _s5_ln(x, batch, w_in, ops, d_skip, w_glu, w_out, g, b, tm=512):
    m = x.shape[0]
    seq = m // batch
    tm = min(tm, seq)
    nt = seq // tm
    ws = [w_in, *ops, d_skip, w_glu, w_out, g, b]
    tok = pl.BlockSpec((tm, D_MODEL), lambda bi, ti: (bi * nt + ti, 0))
    nlb = S5_GB * S5_STATE // 128
    return pl.pallas_call(
        _s5_kernel,
        grid=(batch, nt),
        in_specs=[tok] + [_const_spec(w.shape) for w in ws],
        out_specs=tok,
        out_shape=jax.ShapeDtypeStruct((m, D_MODEL), F32),
        scratch_shapes=[pltpu.VMEM((nlb, tm, 128), F32), pltpu.VMEM((nlb, tm, 128), F32),
                        pltpu.VMEM((nlb, tm // S5_SUB, 128), F32), pltpu.VMEM((nlb, tm // S5_SUB, 128), F32),
                        pltpu.VMEM((S5_NGB, nlb, 1, 128), F32), pltpu.VMEM((S5_NGB, nlb, 1, 128), F32),
                        pltpu.VMEM((tm, D_MODEL), F32)],
        compiler_params=_cparams("parallel", "arbitrary"),
        name="s5_ln",
    )(x, *ws)


def _rw_proj_kernel(x_ref, xp_ref, mu_ref, wr_ref, wk_ref, wv_ref, w0_ref, w1_ref, w2_ref,
                    a0_ref, a1_ref, a2_ref, g1_ref, g2_ref,
                    r_ref, k_ref, v_ref, lw_ref, a_ref, g_ref, *, tiles_per_seq):
    x = x_ref[...]
    first = (pl.program_id(0) % tiles_per_seq) == 0
    prev = jnp.where(first, 0.0, xp_ref[7:8, :])
    rows = lax.broadcasted_iota(jnp.int32, x.shape, 0)
    shifted = jnp.where(rows == 0, prev, pltpu.roll(x, 1, 0))
    xx = shifted - x
    mix = lambda i: (x + xx * mu_ref[i:i + 1, :]).astype(BF16)
    r_ref[...] = _dot(mix(0), wr_ref[...])
    z = w0_ref[...] + _dot_bf(jnp.tanh(_dot(mix(1), w1_ref[...])), w2_ref[...])
    w = -_softplus(-z) - 0.5
    lw_ref[...] = -jnp.exp(w)
    k_ref[...] = _dot(mix(2), wk_ref[...])
    v_ref[...] = _dot(mix(3), wv_ref[...])
    a_ref[...] = jax.nn.sigmoid(a0_ref[...] + _dot_bf(_dot(mix(4), a1_ref[...]), a2_ref[...]))
    g_ref[...] = _dot_bf(jax.nn.sigmoid(_dot(mix(5), g1_ref[...])), g2_ref[...])


def _rw_scan_kernel(r_ref, k_ref, v_ref, lw_ref, a_ref, kk_ref, ka_ref, rk_ref, lg_ref, lb_ref,
                    mask_ref, y_ref, state_ref):
    @pl.when(pl.program_id(2) == 0)
    def _():
        state_ref[...] = jnp.zeros_like(state_ref)

    L, N = RW_CHUNK, RW_HEAD
    tm, width = r_ref.shape
    nc, nh = tm // L, width // N
    nt = (((1,), (1,)), ((), ()))
    tn = (((0,), (0,)), ((), ()))
    strict = mask_ref[0]
    incl = mask_ref[1]
    rr = lax.broadcasted_iota(jnp.int32, (L, L), 0)
    cc = lax.broadcasted_iota(jnp.int32, (L, L), 1)
    eye = jnp.where(rr == cc, 1.0, 0.0).astype(F32)

    lw = lw_ref[...]
    pos = lax.broadcasted_iota(jnp.int32, lw.shape, 0) & (L - 1)
    cum = lw
    step = 1
    while step < L:
        cum = cum + jnp.where(pos >= step, pltpu.roll(cum, step, 0), 0.0)
        step *= 2
    g_in = jnp.exp(cum)
    g_ex = jnp.exp(cum - lw)
    g_inv = jnp.exp(-cum)
    r = r_ref[...]
    k = k_ref[...]
    a = a_ref[...]
    kkr = k * kk_ref[...]
    k = k * (1.0 + (a - 1.0) * ka_ref[...])
    bonus = r * k * rk_ref[...]
    rt_all = r * g_in
    kt_all = k * g_inv

    pre = []
    for hl in range(nh):
        lanes = slice(hl * N, (hl + 1) * N)
        kk = kkr[:, lanes]
        kk = kk / jnp.maximum(jnp.sqrt(jnp.sum(kk * kk, axis=-1, keepdims=True)), 1e-12)
        g_inv_h = g_inv[:, lanes]
        g_in_h = g_in[:, lanes]
        rt = rt_all[:, lanes].astype(BF16)
        at = (-kk * g_ex[:, lanes]).astype(BF16)
        kt = kt_all[:, lanes]
        bt = kk * a[:, lanes] * g_inv_h
        vb = v_ref[:, lanes].astype(BF16)
        per_chunk = []
        for c in range(nc):
            rows = slice(c * L, (c + 1) * L)
            atc, rtc, ktc, btc, vc = at[rows], rt[rows], kt[rows], bt[rows], vb[rows]
            ar = jnp.concatenate([atc, rtc], axis=0)
            gab = lax.dot_general(ar, btc.astype(BF16), nt, preferred_element_type=F32)
            gak = lax.dot_general(ar, ktc.astype(BF16), nt, preferred_element_type=F32)
            a_ab = gab[:L] * strict
            b_rb = (gab[L:] * incl).astype(BF16)
            a_ak = (gak[:L] * strict).astype(BF16)
            b_rk = (gak[L:] * incl).astype(BF16)
            tinv = eye + a_ab * mask_ref[2]
            lvl, bsz = 1, 2
            while bsz < L:
                off = a_ab * mask_ref[2 + lvl]
                tinv = tinv + _dot_bf(tinv, _dot_bf(off, tinv))
                bsz *= 2
                lvl += 1
            tb = tinv.astype(BF16)
            ap = _dot(tb, atc).astype(BF16)
            wloc = _dot(tb, _dot(a_ak, vc).astype(BF16))
            bv = _dot(b_rk, vc)
            g_last = g_in_h[(c + 1) * L - 1:(c + 1) * L, :]
            bg = (btc * g_last).astype(BF16)
            kv = lax.dot_general(vc, (ktc * g_last).astype(BF16), tn, preferred_element_type=F32)
            per_chunk.append((jnp.concatenate([ap, rtc], axis=0), wloc, b_rb, bv, bg, kv, g_last))
        pre.append(per_chunk)

    states = [state_ref[hl] for hl in range(nh)]
    outs = [[] for _ in range(nh)]
    for c in range(nc):
        for hl in range(nh):
            apr, wloc, b_rb, bv, bg, kv, g_last = pre[hl][c]
            state = states[hl]
            both = lax.dot_general(apr, state.astype(BF16), nt, preferred_element_type=F32)
            ub = (both[:L] + wloc).astype(BF16)
            outs[hl].append(both[L:] + _dot(b_rb, ub) + bv)
            states[hl] = state * g_last + lax.dot_general(ub, bg, tn, preferred_element_type=F32) + kv

    for hl in range(nh):
        lanes = slice(hl * N, (hl + 1) * N)
        state_ref[hl] = states[hl]
        o = jnp.concatenate(outs[hl], axis=0)
        mu = jnp.mean(o, axis=-1, keepdims=True)
        d = o - mu
        var = jnp.mean(d * d, axis=-1, keepdims=True)
        o = d * lax.rsqrt(var + RW_GN_EPS) * lg_ref[:, lanes] + lb_ref[:, lanes]
        o = o + jnp.sum(bonus[:, lanes], axis=-1, keepdims=True) * v_ref[:, lanes]
        y_ref[:, lanes] = o


def _rw_out_kernel(x_ref, y_ref, gate_ref, wo_ref, g_ref, b_ref, o_ref):
    out = _dot((y_ref[...] * gate_ref[...]).astype(BF16), wo_ref[...])
    o_ref[...] = _layer_norm(ALPHA * x_ref[...] + out, g_ref[...], b_ref[...])


def _rw_masks():
    L = RW_CHUNK
    i = jnp.arange(L)
    r, c = i[:, None], i[None, :]
    masks = [r > c, r >= c]
    bsz = 1
    while bsz < L:
        masks.append((r // (2 * bsz) == c // (2 * bsz)) & ((r // bsz) % 2 == 1) & ((c // bsz) % 2 == 0))
        bsz *= 2
    return jnp.stack(masks, 0).astype(F32)


def _rwkv_ln(x, batch, p, g, b, tm=512):
    m = x.shape[0]
    seq = m // batch
    tm = min(tm, seq)
    nt = seq // tm
    tok = pl.BlockSpec((tm, D_MODEL), lambda i: (i, 0))
    prev = pl.BlockSpec((8, D_MODEL), lambda i: (jnp.maximum(i * (tm // 8) - 1, 0), 0))
    names = ['mu', 'w_r', 'w_k', 'w_v', 'w0', 'w1', 'w2', 'a0', 'a1', 'a2', 'g1', 'g2']
    ws = [p[n] for n in names]
    act = jax.ShapeDtypeStruct((m, D_MODEL), F32)
    r, k, v, lw, a, gate = pl.pallas_call(
        functools.partial(_rw_proj_kernel, tiles_per_seq=nt),
        grid=(m // tm,),
        in_specs=[tok, prev] + [_const_spec(w.shape) for w in ws],
        out_specs=[tok] * 6,
        out_shape=[act] * 6,
        compiler_params=_cparams("parallel"),
        name="rwkv_proj",
    )(x, x, *ws)

    lanes = RW_HB * RW_HEAD
    ts = min(RW_SCAN_TILE, seq)
    nts = seq // ts
    hblk = pl.BlockSpec((ts, lanes), lambda bi, hi, ti: (bi * nts + ti, hi))
    pblk = pl.BlockSpec((1, lanes), lambda bi, hi, ti: (0, hi))
    masks = _rw_masks()
    y = pl.pallas_call(
        _rw_scan_kernel,
        grid=(batch, RW_HEADS // RW_HB, nts),
        in_specs=[hblk] * 5 + [pblk] * 5 + [_const_spec(masks.shape)],
        out_specs=hblk,
        out_shape=act,
        scratch_shapes=[pltpu.VMEM((RW_HB, RW_HEAD, RW_HEAD), F32)],
        compiler_params=_cparams("parallel", "parallel", "arbitrary"),
        name="rwkv_scan",
    )(r, k, v, lw, a, p['k_k'], p['k_a'], p['r_k'], p['lnx_g'], p['lnx_b'], masks)

    return pl.pallas_call(
        _rw_out_kernel,
        grid=(m // tm,),
        in_specs=[tok, tok, tok, _const_spec(p['w_o'].shape), _const_spec(g.shape), _const_spec(b.shape)],
        out_specs=tok,
        out_shape=act,
        compiler_params=_cparams("parallel"),
        name="rwkv_out_ln",
    )(x, y, gate, p['w_o'], g, b)


def _lru_kernel(x_ref, win_ref, cw_ref, cb_ref, wa_ref, ba_ref, wx_ref, bx_ref, lam_ref, wout_ref,
                g_ref, b_ref, o_ref, xr_ref, a_ref, inp_ref, h_ref):
    W = D_MODEL
    tm = x_ref.shape[0]

    @pl.when(pl.program_id(1) == 0)
    def _():
        xr_ref[...] = jnp.zeros_like(xr_ref)
        h_ref[...] = jnp.zeros_like(h_ref)

    x = x_ref[...]
    proj = _dot(x.astype(BF16), win_ref[...])
    gate = jax.nn.gelu(proj[:, :W])
    xr_ref[0:8, :] = xr_ref[tm:tm + 8, :]
    xr_ref[8:tm + 8, :] = proj[:, W:]
    xc = cb_ref[...] + cw_ref[3:4, :] * xr_ref[8:tm + 8, :]
    for j in range(CONV_WIDTH - 1):
        xc = xc + cw_ref[j:j + 1, :] * xr_ref[pl.ds(5 + j, tm), :]
    xcb = xc.astype(BF16)
    gr = jnp.concatenate([_dot(xcb[:, kb * LRU_BLOCK:(kb + 1) * LRU_BLOCK], wa_ref[kb])
                          for kb in range(LRU_BLOCKS)], axis=-1) + ba_ref[...]
    gi = jnp.concatenate([_dot(xcb[:, kb * LRU_BLOCK:(kb + 1) * LRU_BLOCK], wx_ref[kb])
                          for kb in range(LRU_BLOCKS)], axis=-1) + bx_ref[...]
    log_a = -LRU_C * jax.nn.sigmoid(gr) * _softplus(-lam_ref[...])
    av = jnp.exp(log_a)
    a_ref[...] = av
    inp_ref[...] = jnp.sqrt(1.0 - av * av) * (jax.nn.sigmoid(gi) * xc)

    def block(i, h):
        rows = pl.ds(pl.multiple_of(i * 8, 8), 8)
        ab = a_ref[rows, :]
        bb = inp_ref[rows, :]
        out = []
        for j in range(8):
            h = ab[j:j + 1, :] * h + bb[j:j + 1, :]
            out.append(h)
        inp_ref[rows, :] = jnp.concatenate(out, axis=0)
        return h

    h_ref[...] = lax.fori_loop(0, tm // 8, block, h_ref[...])
    y = _dot((inp_ref[...] * gate).astype(BF16), wout_ref[...])
    o_ref[...] = _layer_norm(ALPHA * x + y, g_ref[...], b_ref[...])


def _lru_ln(x, batch, p, g, b, tm=256):
    m = x.shape[0]
    seq = m // batch
    tm = min(tm, seq)
    nt = seq // tm
    tok = pl.BlockSpec((tm, D_MODEL), lambda bi, ti: (bi * nt + ti, 0))
    names = ['w_in', 'conv_w', 'conv_b', 'w_a', 'b_a', 'w_x', 'b_x', 'lam', 'w_out']
    ws = [p[n] for n in names]
    return pl.pallas_call(
        _lru_kernel,
        grid=(batch, nt),
        in_specs=[tok] + [_const_spec(w.shape) for w in ws] + [_const_spec(g.shape), _const_spec(b.shape)],
        out_specs=tok,
        out_shape=jax.ShapeDtypeStruct((m, D_MODEL), F32),
        scratch_shapes=[pltpu.VMEM((tm + 8, D_MODEL), F32), pltpu.VMEM((tm, D_MODEL), F32),
                        pltpu.VMEM((tm, D_MODEL), F32), pltpu.VMEM((1, D_MODEL), F32)],
        compiler_params=_cparams("parallel", "arbitrary"),
        name="rglru_ln",
    )(x, *ws, g, b)


def kernel(x, ln_g, ln_b, ffn_w1, ffn_w3, ffn_w2, ret_w_in, ret_w_out, s5_w_in, s5_a_re, s5_a_im, s5_b_re, s5_b_im, s5_c_re, s5_c_im, s5_d, s5_log_step, s5_w_glu, s5_w_out, rw_mu, rw_w_r, rw_w_k, rw_w_v, rw_w0, rw_w1, rw_w2, rw_a0, rw_a1, rw_a2, rw_g1, rw_g2, rw_k_k, rw_k_a, rw_r_k, rw_lnx_g, rw_lnx_b, rw_w_o, lru_w_in, lru_conv_w, lru_conv_b, lru_w_a, lru_b_a, lru_w_x, lru_b_x, lru_lambda, lru_w_out):
    batch, seq, d = x.shape
    depth = ln_g.shape[0]
    h = x.reshape(batch * seq, d)
    bf = lambda t: t.astype(BF16)
    row = lambda t: t.reshape(1, -1)
    for i in range(depth):
        m, j = i % N_MIXERS, i // N_MIXERS
        h = _ffn_ln(h, bf(ffn_w1[i, 0]), bf(ffn_w3[i, 0]), bf(ffn_w2[i, 0]), row(ln_g[i, 0]), row(ln_b[i, 0]))
        g, b = row(ln_g[i, 1]), row(ln_b[i, 1])
        if m == 0:
            h = _retention_ln(h, batch, bf(ret_w_in[j]), bf(ret_w_out[j]), g, b)
        elif m == 1:
            ops = _s5_operators(s5_a_re[j], s5_a_im[j], s5_b_re[j], s5_b_im[j], s5_c_re[j], s5_c_im[j],
                                s5_log_step[j])
            h = _s5_ln(h, batch, bf(s5_w_in[j]), ops, row(s5_d[j]), bf(s5_w_glu[j]), bf(s5_w_out[j]), g, b)
        elif m == 2:
            p = dict(mu=rw_mu[j], w_r=bf(rw_w_r[j]), w_k=bf(rw_w_k[j]), w_v=bf(rw_w_v[j]),
                     w0=row(rw_w0[j]), w1=bf(rw_w1[j]), w2=bf(rw_w2[j]),
                     a0=row(rw_a0[j]), a1=bf(rw_a1[j]), a2=bf(rw_a2[j]), g1=bf(rw_g1[j]), g2=bf(rw_g2[j]),
                     k_k=row(rw_k_k[j]), k_a=row(rw_k_a[j]), r_k=row(rw_r_k[j]),
                     lnx_g=row(rw_lnx_g[j]), lnx_b=row(rw_lnx_b[j]), w_o=bf(rw_w_o[j]))
            h = _rwkv_ln(h, batch, p, g, b)
        else:
            p = dict(w_in=bf(lru_w_in[j]), conv_w=lru_conv_w[j], conv_b=row(lru_conv_b[j]),
                     w_a=bf(lru_w_a[j]), b_a=row(lru_b_a[j]), w_x=bf(lru_w_x[j]), b_x=row(lru_b_x[j]),
                     lam=row(lru_lambda[j]), w_out=bf(lru_w_out[j]))
            h = _lru_ln(h, batch, p, g, b)
        h = _ffn_ln(h, bf(ffn_w1[i, 1]), bf(ffn_w3[i, 1]), bf(ffn_w2[i, 1]), row(ln_g[i, 2]), row(ln_b[i, 2]))
    return h.reshape(batch, seq, d)
```

```python
import functools
import math

import jax
import jax.numpy as jnp
from jax import lax
from jax.experimental import pallas as pl
from jax.experimental.pallas import tpu as pltpu

F32 = jnp.float32
BF16 = jnp.bfloat16

D_MODEL = 1024
DEPTH = 4
N_MIXERS = 4
ALPHA = (2 * DEPTH) ** 0.25
LN_EPS = 1e-5
D_FF = 2816
FFN_HALF = 0.5

RET_HEADS = 4
RET_DK = 256
RET_DV = 512
ROPE_BASE = 10000.0
RET_GN_EPS = 1e-5

S5_GROUPS = 64
S5_GROUP = 16
S5_STATE = 64
S5_SUB = 16
S5_GB = 16
S5_NGB = S5_GROUPS // S5_GB

RW_HEADS = 16
RW_HEAD = 64
RW_GN_EPS = 64e-5
RW_CHUNK = 64
RW_HB = 4
RW_SCAN_TILE = 256

LRU_BLOCKS = 4
LRU_BLOCK = 256
CONV_WIDTH = 4
LRU_C = 8.0

VMEM_LIMIT = 56 * 1024 * 1024


def _cparams(*sem):
    return pltpu.CompilerParams(dimension_semantics=sem, vmem_limit_bytes=VMEM_LIMIT)


def _const_spec(shape):
    nd = len(shape)
    return pl.BlockSpec(shape, lambda *_: (0,) * nd, pipeline_mode=pl.Buffered(1))


def _layer_norm(r, g, b):
    mu = jnp.mean(r, axis=-1, keepdims=True)
    d = r - mu
    var = jnp.mean(d * d, axis=-1, keepdims=True)
    return d * lax.rsqrt(var + LN_EPS) * g + b


def _dot(a, b):
    return jnp.dot(a, b, preferred_element_type=F32)


def _dot_bf(a, b):
    return jnp.dot(a.astype(BF16), b.astype(BF16), preferred_element_type=F32)


def _softplus(z):
    return jnp.maximum(z, 0.0) + jnp.log(1.0 + jnp.exp(-jnp.abs(z)))


def _ffn_kernel(x_ref, w1_ref, w3_ref, w2_ref, g_ref, b_ref, o_ref):
    x = x_ref[...]
    xb = x.astype(BF16)
    h1 = _dot(xb, w1_ref[...])
    h3 = _dot(xb, w3_ref[...])
    act = (h1 * jax.nn.sigmoid(h1) * h3).astype(BF16)
    y = _dot(act, w2_ref[...])
    o_ref[...] = _layer_norm(ALPHA * x + FFN_HALF * y, g_ref[...], b_ref[...])


def _ffn_ln(x, w1, w3, w2, g, b, tm=512):
    m = x.shape[0]
    tm = min(tm, m)
    tok = pl.BlockSpec((tm, D_MODEL), lambda i: (i, 0))
    return pl.pallas_call(
        _ffn_kernel,
        grid=(m // tm,),
        in_specs=[tok, _const_spec(w1.shape), _const_spec(w3.shape), _const_spec(w2.shape),
                  _const_spec(g.shape), _const_spec(b.shape)],
        out_specs=tok,
        out_shape=jax.ShapeDtypeStruct((m, D_MODEL), F32),
        compiler_params=_cparams("parallel"),
        name="ffn_ln",
    )(x, w1, w3, w2, g, b)


def _rotate(t1, t2, cos, sin):
    return jnp.concatenate([t1 * cos - t2 * sin, t1 * sin + t2 * cos], axis=-1)


def _ret_kernel(x_ref, cos_ref, sin_ref, win_ref, wout_ref, dmat_ref, qd_ref, kd_ref,
                g_ref, b_ref, o_ref, state_ref):
    @pl.when(pl.program_id(1) == 0)
    def _():
        state_ref[...] = jnp.zeros_like(state_ref)

    H, DK, DV = RET_HEADS, RET_DK, RET_DV
    half = DK // 2
    x = x_ref[...]
    tc = x.shape[0]
    proj = _dot(x.astype(BF16), win_ref[...])
    cos = cos_ref[...]
    sin = sin_ref[...]
    y = jnp.zeros_like(x)
    for h in range(H):
        q0 = h * DK
        k0 = H * DK + h * DK
        v0 = 2 * H * DK + h * DV
        g0 = 2 * H * DK + H * DV + h * DV
        qh = _rotate(proj[:, q0:q0 + half], proj[:, q0 + half:q0 + DK], cos, sin)
        kh = _rotate(proj[:, k0:k0 + half], proj[:, k0 + half:k0 + DK], cos, sin) * (DK ** -0.5)
        vh = proj[:, v0:v0 + DV].astype(BF16)
        gate = proj[:, g0:g0 + DV]
        s = lax.dot_general(qh.astype(BF16), kh.astype(BF16), (((1,), (1,)), ((), ())),
                            preferred_element_type=F32) * dmat_ref[h]
        state = state_ref[h]
        o = _dot(s.astype(BF16), vh) + _dot_bf(qh * qd_ref[h], state)
        kdec = (kh * kd_ref[h]).T.astype(BF16)
        chunk_decay = math.exp(tc * math.log1p(-(2.0 ** (-5.0 - h))))
        state_ref[h] = state * chunk_decay + _dot(kdec, vh)
        mu = jnp.mean(o, axis=-1, keepdims=True)
        d = o - mu
        var = jnp.mean(d * d, axis=-1, keepdims=True)
        o = d * lax.rsqrt(var + RET_GN_EPS)
        o = gate * jax.nn.sigmoid(gate) * o
        y = y + _dot(o.astype(BF16), wout_ref[h * DV:(h + 1) * DV, :])
    o_ref[...] = _layer_norm(ALPHA * x + y, g_ref[...], b_ref[...])


def _retention_ln(x, batch, w_in, w_out, g, b, tc=256):
    m = x.shape[0]
    seq = m // batch
    tc = min(tc, seq)
    nt = seq // tc
    H, DK = RET_HEADS, RET_DK
    half = DK // 2
    inv = ROPE_BASE ** (-jnp.arange(half, dtype=F32) / half)
    ang = jnp.arange(seq, dtype=F32)[:, None] * inv[None, :]
    cos, sin = jnp.cos(ang), jnp.sin(ang)
    log_gamma = jnp.log1p(-jnp.power(2.0, -5.0 - jnp.arange(H, dtype=F32)))
    pos = jnp.arange(tc, dtype=F32)
    rel = pos[:, None] - pos[None, :]
    dmat = jnp.where(rel >= 0, jnp.exp(jnp.maximum(rel, 0.0)[None] * log_gamma[:, None, None]), 0.0)
    qd = jnp.broadcast_to(jnp.exp((pos + 1.0)[None, :, None] * log_gamma[:, None, None]), (H, tc, DK))
    kd = jnp.broadcast_to(jnp.exp((tc - 1.0 - pos)[None, :, None] * log_gamma[:, None, None]), (H, tc, DK))

    tok = pl.BlockSpec((tc, D_MODEL), lambda bi, ti: (bi * nt + ti, 0))
    rope = pl.BlockSpec((tc, half), lambda bi, ti: (ti, 0))
    return pl.pallas_call(
        _ret_kernel,
        grid=(batch, nt),
        in_specs=[tok, rope, rope, _const_spec(w_in.shape), _const_spec(w_out.shape),
                  _const_spec(dmat.shape), _const_spec(qd.shape), _const_spec(kd.shape),
                  _const_spec(g.shape), _const_spec(b.shape)],
        out_specs=tok,
        out_shape=jax.ShapeDtypeStruct((m, D_MODEL), F32),
        scratch_shapes=[pltpu.VMEM((H, DK, RET_DV), F32)],
        compiler_params=_cparams("parallel", "arbitrary"),
        name="retention_ln",
    )(x, cos, sin, w_in, w_out, dmat, qd, kd, g, b)


def _s5_kernel(x_ref, perm_ref, permt_ref, win_ref, bre_ref, bim_ref, cre_ref, cim_ref, pwr_ref, pwi_ref, d_ref,
               wglu_ref, wout_ref, g_ref, b_ref, o_ref, hre_ref, him_ref, ere_ref, eim_ref,
               sre_ref, sim_ref, y_ref):
    @pl.when(pl.program_id(1) == 0)
    def _():
        sre_ref[...] = jnp.zeros_like(sre_ref)
        sim_ref[...] = jnp.zeros_like(sim_ref)

    L = S5_SUB
    tm = x_ref.shape[0]
    nr = tm // L
    cols = S5_GB * S5_GROUP
    x = x_ref[...]
    xp = _dot(perm_ref[...], x.astype(BF16)).astype(BF16)
    u = _dot(xp, win_ref[...])
    ub = u.astype(BF16)
    for gb in range(S5_NGB):
        ug = ub[:, gb * cols:(gb + 1) * cols]
        hre_ref[...] = _dot(ug, bre_ref[gb])
        him_ref[...] = _dot(ug, bim_ref[gb])
        sub = lambda j: slice(j * nr, (j + 1) * nr)
        ar = pwr_ref[gb, 1:2, :]
        ai = pwi_ref[gb, 1:2, :]
        hr = hre_ref[sub(0), :]
        hi = him_ref[sub(0), :]
        for j in range(1, L):
            hr, hi = (ar * hr - ai * hi + hre_ref[sub(j), :], ar * hi + ai * hr + him_ref[sub(j), :])
            hre_ref[sub(j), :] = hr
            him_ref[sub(j), :] = hi
        ere_ref[...] = hr
        eim_ref[...] = hi
        alr = pwr_ref[gb, L:L + 1, :]
        ali = pwi_ref[gb, L:L + 1, :]

        def row(c, carry):
            sr, si = carry
            er = ere_ref[pl.ds(c, 1), :]
            ei = eim_ref[pl.ds(c, 1), :]
            ere_ref[pl.ds(c, 1), :] = sr
            eim_ref[pl.ds(c, 1), :] = si
            return alr * sr - ali * si + er, alr * si + ali * sr + ei

        sr, si = lax.fori_loop(0, nr, row, (sre_ref[gb], sim_ref[gb]))
        sre_ref[gb] = sr
        sim_ref[gb] = si
        pr_in = ere_ref[...]
        pi_in = eim_ref[...]
        for j in range(L):
            pr = pwr_ref[gb, j + 1:j + 2, :]
            pi = pwi_ref[gb, j + 1:j + 2, :]
            hre_ref[sub(j), :] = hre_ref[sub(j), :] + (pr * pr_in - pi * pi_in)
            him_ref[sub(j), :] = him_ref[sub(j), :] + (pr * pi_in + pi * pr_in)
        y_ref[:, gb * cols:(gb + 1) * cols] = (_dot_bf(hre_ref[...], cre_ref[gb])
                                               + _dot_bf(him_ref[...], cim_ref[gb]))
    yv = y_ref[...] + d_ref[...] * u
    act = jax.nn.gelu(yv)
    z = act * jax.nn.sigmoid(_dot(act.astype(BF16), wglu_ref[...]))
    out = _dot(z.astype(BF16), wout_ref[...])
    hi16 = out.astype(BF16)
    lo16 = (out - hi16.astype(F32)).astype(BF16)
    out = _dot(permt_ref[...], hi16) + _dot(permt_ref[...], lo16)
    o_ref[...] = _layer_norm(ALPHA * x + out, g_ref[...], b_ref[...])


def _s5_operators(a_re, a_im, b_re, b_im, c_re, c_im, log_step):
    L = S5_SUB
    G, N, P = b_re.shape
    dt = jnp.exp(log_step)[:, None]
    mag = jnp.exp(dt * a_re)
    abar_re = mag * jnp.cos(dt * a_im)
    abar_im = mag * jnp.sin(dt * a_im)
    den = a_re * a_re + a_im * a_im
    f_re = ((abar_re - 1.0) * a_re + abar_im * a_im) / den
    f_im = (abar_im * a_re - (abar_re - 1.0) * a_im) / den
    bb_re = f_re[..., None] * b_re - f_im[..., None] * b_im
    bb_im = f_re[..., None] * b_im + f_im[..., None] * b_re
    pr, pi = [jnp.ones_like(abar_re)], [jnp.zeros_like(abar_re)]
    for _ in range(L):
        pr.append(pr[-1] * abar_re - pi[-1] * abar_im)
        pi.append(pr[-2] * abar_im + pi[-1] * abar_re)
    gbn, ngb = S5_GB, S5_NGB
    eye = jnp.eye(gbn, dtype=F32)
    blockdiag_in = lambda t: jnp.einsum('bgnp,gh->bgphn', t.reshape(ngb, gbn, N, P), eye).reshape(
        ngb, gbn * P, gbn * N)
    blockdiag_out = lambda t: jnp.einsum('bgpn,gh->bgnhp', t.reshape(ngb, gbn, P, N), eye).reshape(
        ngb, gbn * N, gbn * P)
    planes = lambda t: jnp.stack(t, 0).reshape(L + 1, ngb, gbn * N).transpose(1, 0, 2)
    return (blockdiag_in(bb_re).astype(BF16), blockdiag_in(bb_im).astype(BF16),
            blockdiag_out(c_re).astype(BF16), blockdiag_out(-c_im).astype(BF16), planes(pr), planes(pi))


def _s5_ln(x, batch, w_in, ops, d_skip, w_glu, w_out, g, b, tm=512):
    m = x.shape[0]
    seq = m // batch
    tm = min(tm, seq)
    nt = seq // tm
    nr = tm // S5_SUB
    rows = jnp.arange(tm)
    perm = (rows[None, :] == (rows[:, None] % nr) * S5_SUB + rows[:, None] // nr).astype(BF16)
    ws = [perm, perm.T, w_in, *ops, d_skip, w_glu, w_out, g, b]
    tok = pl.BlockSpec((tm, D_MODEL), lambda bi, ti: (bi * nt + ti, 0))
    wide = S5_GB * S5_STATE
    return pl.pallas_call(
        _s5_kernel,
        grid=(batch, nt),
        in_specs=[tok] + [_const_spec(w.shape) for w in ws],
        out_specs=tok,
        out_shape=jax.ShapeDtypeStruct((m, D_MODEL), F32),
        scratch_shapes=[pltpu.VMEM((tm, wide), F32), pltpu.VMEM((tm, wide), F32),
                        pltpu.VMEM((nr, wide), F32), pltpu.VMEM((nr, wide), F32),
                        pltpu.VMEM((S5_NGB, 1, wide), F32), pltpu.VMEM((S5_NGB, 1, wide), F32),
                        pltpu.VMEM((tm, D_MODEL), F32)],
        compiler_params=_cparams("parallel", "arbitrary"),
        name="s5_ln",
    )(x, *ws)


def _rw_proj_kernel(x_ref, xp_ref, mu_ref, wr_ref, wk_ref, wv_ref, w0_ref, w1_ref, w2_ref,
                    a0_ref, a1_ref, a2_ref, g1_ref, g2_ref,
                    r_ref, k_ref, v_ref, lw_ref, a_ref, g_ref, *, tiles_per_seq):
    x = x_ref[...]
    first = (pl.program_id(0) % tiles_per_seq) == 0
    prev = jnp.where(first, 0.0, xp_ref[7:8, :])
    rows = lax.broadcasted_iota(jnp.int32, x.shape, 0)
    shifted = jnp.where(rows == 0, prev, pltpu.roll(x, 1, 0))
    xx = shifted - x
    mix = lambda i: (x + xx * mu_ref[i:i + 1, :]).astype(BF16)
    r_ref[...] = _dot(mix(0), wr_ref[...])
    z = w0_ref[...] + _dot_bf(jnp.tanh(_dot(mix(1), w1_ref[...])), w2_ref[...])
    w = -_softplus(-z) - 0.5
    lw_ref[...] = -jnp.exp(w)
    k_ref[...] = _dot(mix(2), wk_ref[...])
    v_ref[...] = _dot(mix(3), wv_ref[...])
    a_ref[...] = jax.nn.sigmoid(a0_ref[...] + _dot_bf(_dot(mix(4), a1_ref[...]), a2_ref[...]))
    g_ref[...] = _dot_bf(jax.nn.sigmoid(_dot(mix(5), g1_ref[...])), g2_ref[...])


def _rw_scan_kernel(r_ref, k_ref, v_ref, lw_ref, a_ref, kk_ref, ka_ref, rk_ref, lg_ref, lb_ref,
                    mask_ref, y_ref, state_ref):
    @pl.when(pl.program_id(2) == 0)
    def _():
        state_ref[...] = jnp.zeros_like(state_ref)

    L, N = RW_CHUNK, RW_HEAD
    tm, width = r_ref.shape
    nc, nh = tm // L, width // N
    nt = (((1,), (1,)), ((), ()))
    tn = (((0,), (0,)), ((), ()))
    strict = mask_ref[0]
    incl = mask_ref[1]
    rr = lax.broadcasted_iota(jnp.int32, (L, L), 0)
    cc = lax.broadcasted_iota(jnp.int32, (L, L), 1)
    eye = jnp.where(rr == cc, 1.0, 0.0).astype(F32)

    lw = lw_ref[...]
    pos = lax.broadcasted_iota(jnp.int32, lw.shape, 0) & (L - 1)
    cum = lw
    step = 1
    while step < L:
        cum = cum + jnp.where(pos >= step, pltpu.roll(cum, step, 0), 0.0)
        step *= 2
    g_in = jnp.exp(cum)
    g_ex = jnp.exp(cum - lw)
    g_inv = jnp.exp(-cum)
    r = r_ref[...]
    k = k_ref[...]
    a = a_ref[...]
    kkr = k * kk_ref[...]
    k = k * (1.0 + (a - 1.0) * ka_ref[...])
    bonus = r * k * rk_ref[...]
    rt_all = r * g_in
    kt_all = k * g_inv

    chains = [(hl, c) for hl in range(nh) for c in range(nc)]
    atc, rtc, ktc, btc, vc, gl = {}, {}, {}, {}, {}, {}
    for hl in range(nh):
        lanes = slice(hl * N, (hl + 1) * N)
        kk = kkr[:, lanes]
        kk = kk / jnp.maximum(jnp.sqrt(jnp.sum(kk * kk, axis=-1, keepdims=True)), 1e-12)
        rt = rt_all[:, lanes].astype(BF16)
        at = (-kk * g_ex[:, lanes]).astype(BF16)
        kt = kt_all[:, lanes]
        bt = kk * a[:, lanes] * g_inv[:, lanes]
        vb = v_ref[:, lanes].astype(BF16)
        for c in range(nc):
            rows = slice(c * L, (c + 1) * L)
            atc[hl, c], rtc[hl, c], ktc[hl, c], btc[hl, c], vc[hl, c] = (
                at[rows], rt[rows], kt[rows], bt[rows], vb[rows])
            gl[hl, c] = g_in[(c + 1) * L - 1:(c + 1) * L, lanes]
    ar = {i: jnp.concatenate([atc[i], rtc[i]], axis=0) for i in chains}
    gab = {i: lax.dot_general(ar[i], btc[i].astype(BF16), nt, preferred_element_type=F32) for i in chains}
    gak = {i: lax.dot_general(ar[i], ktc[i].astype(BF16), nt, preferred_element_type=F32) for i in chains}
    a_ab = {i: gab[i][:L] * strict for i in chains}
    b_rb = {i: (gab[i][L:] * incl).astype(BF16) for i in chains}
    a_ak = {i: (gak[i][:L] * strict).astype(BF16) for i in chains}
    b_rk = {i: (gak[i][L:] * incl).astype(BF16) for i in chains}
    tinv = {i: eye + a_ab[i] * mask_ref[2] for i in chains}
    lvl, bsz = 1, 2
    while bsz < L:
        mask = mask_ref[2 + lvl]
        half = {i: _dot_bf(a_ab[i] * mask, tinv[i]) for i in chains}
        tinv = {i: tinv[i] + _dot_bf(tinv[i], half[i]) for i in chains}
        bsz *= 2
        lvl += 1
    tb = {i: tinv[i].astype(BF16) for i in chains}
    ap = {i: _dot(tb[i], atc[i]).astype(BF16) for i in chains}
    akv = {i: _dot(a_ak[i], vc[i]).astype(BF16) for i in chains}
    wloc = {i: _dot(tb[i], akv[i]) for i in chains}
    bv = {i: _dot(b_rk[i], vc[i]) for i in chains}
    bg = {i: (btc[i] * gl[i]).astype(BF16) for i in chains}
    kv = {i: lax.dot_general(vc[i], (ktc[i] * gl[i]).astype(BF16), tn, preferred_element_type=F32)
          for i in chains}
    apr = {i: jnp.concatenate([ap[i], rtc[i]], axis=0) for i in chains}

    states = [state_ref[hl] for hl in range(nh)]
    outs = [[] for _ in range(nh)]
    for c in range(nc):
        both = [lax.dot_general(apr[hl, c], states[hl].astype(BF16), nt, preferred_element_type=F32)
                for hl in range(nh)]
        ub = [(both[hl][:L] + wloc[hl, c]).astype(BF16) for hl in range(nh)]
        for hl in range(nh):
            states[hl] = (states[hl] * gl[hl, c] + kv[hl, c]
                          + lax.dot_general(ub[hl], bg[hl, c], tn, preferred_element_type=F32))
        for hl in range(nh):
            outs[hl].append(both[hl][L:] + _dot(b_rb[hl, c], ub[hl]) + bv[hl, c])

    for hl in range(nh):
        lanes = slice(hl * N, (hl + 1) * N)
        state_ref[hl] = states[hl]
        o = jnp.concatenate(outs[hl], axis=0)
        mu = jnp.mean(o, axis=-1, keepdims=True)
        d = o - mu
        var = jnp.mean(d * d, axis=-1, keepdims=True)
        o = d * lax.rsqrt(var + RW_GN_EPS) * lg_ref[:, lanes] + lb_ref[:, lanes]
        o = o + jnp.sum(bonus[:, lanes], axis=-1, keepdims=True) * v_ref[:, lanes]
        y_ref[:, lanes] = o


def _rw_out_kernel(x_ref, y_ref, gate_ref, wo_ref, g_ref, b_ref, o_ref):
    out = _dot((y_ref[...] * gate_ref[...]).astype(BF16), wo_ref[...])
    o_ref[...] = _layer_norm(ALPHA * x_ref[...] + out, g_ref[...], b_ref[...])


def _rw_masks():
    L = RW_CHUNK
    i = jnp.arange(L)
    r, c = i[:, None], i[None, :]
    masks = [r > c, r >= c]
    bsz = 1
    while bsz < L:
        masks.append((r // (2 * bsz) == c // (2 * bsz)) & ((r // bsz) % 2 == 1) & ((c // bsz) % 2 == 0))
        bsz *= 2
    return jnp.stack(masks, 0).astype(F32)


def _rwkv_ln(x, batch, p, g, b, tm=512):
    m = x.shape[0]
    seq = m // batch
    tm = min(tm, seq)
    nt = seq // tm
    tok = pl.BlockSpec((tm, D_MODEL), lambda i: (i, 0))
    prev = pl.BlockSpec((8, D_MODEL), lambda i: (jnp.maximum(i * (tm // 8) - 1, 0), 0))
    names = ['mu', 'w_r', 'w_k', 'w_v', 'w0', 'w1', 'w2', 'a0', 'a1', 'a2', 'g1', 'g2']
    ws = [p[n] for n in names]
    act = jax.ShapeDtypeStruct((m, D_MODEL), F32)
    r, k, v, lw, a, gate = pl.pallas_call(
        functools.partial(_rw_proj_kernel, tiles_per_seq=nt),
        grid=(m // tm,),
        in_specs=[tok, prev] + [_const_spec(w.shape) for w in ws],
        out_specs=[tok] * 6,
        out_shape=[act] * 6,
        compiler_params=_cparams("parallel"),
        name="rwkv_proj",
    )(x, x, *ws)

    lanes = RW_HB * RW_HEAD
    ts = min(RW_SCAN_TILE, seq)
    nts = seq // ts
    hblk = pl.BlockSpec((ts, lanes), lambda bi, hi, ti: (bi * nts + ti, hi))
    pblk = pl.BlockSpec((1, lanes), lambda bi, hi, ti: (0, hi))
    masks = _rw_masks()
    y = pl.pallas_call(
        _rw_scan_kernel,
        grid=(batch, RW_HEADS // RW_HB, nts),
        in_specs=[hblk] * 5 + [pblk] * 5 + [_const_spec(masks.shape)],
        out_specs=hblk,
        out_shape=act,
        scratch_shapes=[pltpu.VMEM((RW_HB, RW_HEAD, RW_HEAD), F32)],
        compiler_params=_cparams("parallel", "parallel", "arbitrary"),
        name="rwkv_scan",
    )(r, k, v, lw, a, p['k_k'], p['k_a'], p['r_k'], p['lnx_g'], p['lnx_b'], masks)

    return pl.pallas_call(
        _rw_out_kernel,
        grid=(m // tm,),
        in_specs=[tok, tok, tok, _const_spec(p['w_o'].shape), _const_spec(g.shape), _const_spec(b.shape)],
        out_specs=tok,
        out_shape=act,
        compiler_params=_cparams("parallel"),
        name="rwkv_out_ln",
    )(x, y, gate, p['w_o'], g, b)


def _lru_kernel(x_ref, win_ref, cw_ref, cb_ref, wa_ref, ba_ref, wx_ref, bx_ref, lam_ref, wout_ref,
                g_ref, b_ref, o_ref, xr_ref, a_ref, inp_ref, h_ref):
    W = D_MODEL
    tm = x_ref.shape[0]

    @pl.when(pl.program_id(1) == 0)
    def _():
        xr_ref[...] = jnp.zeros_like(xr_ref)
        h_ref[...] = jnp.zeros_like(h_ref)

    x = x_ref[...]
    proj = _dot(x.astype(BF16), win_ref[...])
    gate = jax.nn.gelu(proj[:, :W])
    xr_ref[0:8, :] = xr_ref[tm:tm + 8, :]
    xr_ref[8:tm + 8, :] = proj[:, W:]
    xc = cb_ref[...] + cw_ref[3:4, :] * xr_ref[8:tm + 8, :]
    for j in range(CONV_WIDTH - 1):
        xc = xc + cw_ref[j:j + 1, :] * xr_ref[pl.ds(5 + j, tm), :]
    xcb = xc.astype(BF16)
    gr = jnp.concatenate([_dot(xcb[:, kb * LRU_BLOCK:(kb + 1) * LRU_BLOCK], wa_ref[kb])
                          for kb in range(LRU_BLOCKS)], axis=-1) + ba_ref[...]
    gi = jnp.concatenate([_dot(xcb[:, kb * LRU_BLOCK:(kb + 1) * LRU_BLOCK], wx_ref[kb])
                          for kb in range(LRU_BLOCKS)], axis=-1) + bx_ref[...]
    log_a = -LRU_C * jax.nn.sigmoid(gr) * _softplus(-lam_ref[...])
    av = jnp.exp(log_a)
    a_ref[...] = av
    inp_ref[...] = jnp.sqrt(1.0 - av * av) * (jax.nn.sigmoid(gi) * xc)

    def block(i, h):
        rows = pl.ds(pl.multiple_of(i * 8, 8), 8)
        ab = a_ref[rows, :]
        bb = inp_ref[rows, :]
        out = []
        for j in range(8):
            h = ab[j:j + 1, :] * h + bb[j:j + 1, :]
            out.append(h)
        inp_ref[rows, :] = jnp.concatenate(out, axis=0)
        return h

    h_ref[...] = lax.fori_loop(0, tm // 8, block, h_ref[...])
    y = _dot((inp_ref[...] * gate).astype(BF16), wout_ref[...])
    o_ref[...] = _layer_norm(ALPHA * x + y, g_ref[...], b_ref[...])


def _lru_ln(x, batch, p, g, b, tm=256):
    m = x.shape[0]
    seq = m // batch
    tm = min(tm, seq)
    nt = seq // tm
    tok = pl.BlockSpec((tm, D_MODEL), lambda bi, ti: (bi * nt + ti, 0))
    names = ['w_in', 'conv_w', 'conv_b', 'w_a', 'b_a', 'w_x', 'b_x', 'lam', 'w_out']
    ws = [p[n] for n in names]
    return pl.pallas_call(
        _lru_kernel,
        grid=(batch, nt),
        in_specs=[tok] + [_const_spec(w.shape) for w in ws] + [_const_spec(g.shape), _const_spec(b.shape)],
        out_specs=tok,
        out_shape=jax.ShapeDtypeStruct((m, D_MODEL), F32),
        scratch_shapes=[pltpu.VMEM((tm + 8, D_MODEL), F32), pltpu.VMEM((tm, D_MODEL), F32),
                        pltpu.VMEM((tm, D_MODEL), F32), pltpu.VMEM((1, D_MODEL), F32)],
        compiler_params=_cparams("parallel", "arbitrary"),
        name="rglru_ln",
    )(x, *ws, g, b)


def kernel(x, ln_g, ln_b, ffn_w1, ffn_w3, ffn_w2, ret_w_in, ret_w_out, s5_w_in, s5_a_re, s5_a_im, s5_b_re, s5_b_im, s5_c_re, s5_c_im, s5_d, s5_log_step, s5_w_glu, s5_w_out, rw_mu, rw_w_r, rw_w_k, rw_w_v, rw_w0, rw_w1, rw_w2, rw_a0, rw_a1, rw_a2, rw_g1, rw_g2, rw_k_k, rw_k_a, rw_r_k, rw_lnx_g, rw_lnx_b, rw_w_o, lru_w_in, lru_conv_w, lru_conv_b, lru_w_a, lru_b_a, lru_w_x, lru_b_x, lru_lambda, lru_w_out):
    batch, seq, d = x.shape
    depth = ln_g.shape[0]
    h = x.reshape(batch * seq, d)
    bf = lambda t: t.astype(BF16)
    row = lambda t: t.reshape(1, -1)
    for i in range(depth):
        m, j = i % N_MIXERS, i // N_MIXERS
        h = _ffn_ln(h, bf(ffn_w1[i, 0]), bf(ffn_w3[i, 0]), bf(ffn_w2[i, 0]), row(ln_g[i, 0]), row(ln_b[i, 0]))
        g, b = row(ln_g[i, 1]), row(ln_b[i, 1])
        if m == 0:
            h = _retention_ln(h, batch, bf(ret_w_in[j]), bf(ret_w_out[j]), g, b)
        elif m == 1:
            ops = _s5_operators(s5_a_re[j], s5_a_im[j], s5_b_re[j], s5_b_im[j], s5_c_re[j], s5_c_im[j],
                                s5_log_step[j])
            h = _s5_ln(h, batch, bf(s5_w_in[j]), ops, row(s5_d[j]), bf(s5_w_glu[j]), bf(s5_w_out[j]), g, b)
        elif m == 2:
            p = dict(mu=rw_mu[j], w_r=bf(rw_w_r[j]), w_k=bf(rw_w_k[j]), w_v=bf(rw_w_v[j]),
                     w0=row(rw_w0[j]), w1=bf(rw_w1[j]), w2=bf(rw_w2[j]),
                     a0=row(rw_a0[j]), a1=bf(rw_a1[j]), a2=bf(rw_a2[j]), g1=bf(rw_g1[j]), g2=bf(rw_g2[j]),
                     k_k=row(rw_k_k[j]), k_a=row(rw_k_a[j]), r_k=row(rw_r_k[j]),
                     lnx_g=row(rw_lnx_g[j]), lnx_b=row(rw_lnx_b[j]), w_o=bf(rw_w_o[j]))
            h = _rwkv_ln(h, batch, p, g, b)
        else:
            p = dict(w_in=bf(lru_w_in[j]), conv_w=lru_conv_w[j], conv_b=row(lru_conv_b[j]),
                     w_a=bf(lru_w_a[j]), b_a=row(lru_b_a[j]), w_x=bf(lru_w_x[j]), b_x=row(lru_b_x[j]),
                     lam=row(lru_lambda[j]), w_out=bf(lru_w_out[j]))
            h = _lru_ln(h, batch, p, g, b)
        h = _ffn_ln(h, bf(ffn_w1[i, 1]), bf(ffn_w3[i, 1]), bf(ffn_w2[i, 1]), row(ln_g[i, 2]), row(ln_b[i, 2]))
    return h.reshape(batch, seq, d)
```

```python
import functools
import math

import jax
import jax.numpy as jnp
from jax import lax
from jax.experimental import pallas as pl
from jax.experimental.pallas import tpu as pltpu

F32 = jnp.float32
BF16 = jnp.bfloat16

D_MODEL = 1024
DEPTH = 4
N_MIXERS = 4
ALPHA = (2 * DEPTH) ** 0.25
LN_EPS = 1e-5
D_FF = 2816
FFN_HALF = 0.5

RET_HEADS = 4
RET_DK = 256
RET_DV = 512
ROPE_BASE = 10000.0
RET_GN_EPS = 1e-5

S5_GROUPS = 64
S5_GROUP = 16
S5_STATE = 64
S5_SUB = 16
S5_GB = 16
S5_NGB = S5_GROUPS // S5_GB

RW_HEADS = 16
RW_HEAD = 64
RW_GN_EPS = 64e-5
RW_CHUNK = 64
RW_HB = 8
RW_SCAN_TILE = 256

LRU_BLOCKS = 4
LRU_BLOCK = 256
CONV_WIDTH = 4
LRU_C = 8.0

VMEM_LIMIT = 56 * 1024 * 1024


def _cparams(*sem):
    return pltpu.CompilerParams(dimension_semantics=sem, vmem_limit_bytes=VMEM_LIMIT)


def _const_spec(shape):
    nd = len(shape)
    return pl.BlockSpec(shape, lambda *_: (0,) * nd, pipeline_mode=pl.Buffered(1))


def _layer_norm(r, g, b):
    mu = jnp.mean(r, axis=-1, keepdims=True)
    d = r - mu
    var = jnp.mean(d * d, axis=-1, keepdims=True)
    return d * lax.rsqrt(var + LN_EPS) * g + b


def _dot(a, b):
    return jnp.dot(a, b, preferred_element_type=F32)


def _dot_bf(a, b):
    return jnp.dot(a.astype(BF16), b.astype(BF16), preferred_element_type=F32)


def _softplus(z):
    return jnp.maximum(z, 0.0) + jnp.log(1.0 + jnp.exp(-jnp.abs(z)))


def _ffn_kernel(x_ref, w1_ref, w3_ref, w2_ref, g_ref, b_ref, o_ref):
    x = x_ref[...]
    xb = x.astype(BF16)
    h1 = _dot(xb, w1_ref[...])
    h3 = _dot(xb, w3_ref[...])
    act = (h1 * jax.nn.sigmoid(h1) * h3).astype(BF16)
    y = _dot(act, w2_ref[...])
    o_ref[...] = _layer_norm(ALPHA * x + FFN_HALF * y, g_ref[...], b_ref[...])


def _ffn_ln(x, w1, w3, w2, g, b, tm=512):
    m = x.shape[0]
    tm = min(tm, m)
    tok = pl.BlockSpec((tm, D_MODEL), lambda i: (i, 0))
    return pl.pallas_call(
        _ffn_kernel,
        grid=(m // tm,),
        in_specs=[tok, _const_spec(w1.shape), _const_spec(w3.shape), _const_spec(w2.shape),
                  _const_spec(g.shape), _const_spec(b.shape)],
        out_specs=tok,
        out_shape=jax.ShapeDtypeStruct((m, D_MODEL), F32),
        compiler_params=_cparams("parallel"),
        name="ffn_ln",
    )(x, w1, w3, w2, g, b)


def _rotate(t1, t2, cos, sin):
    return jnp.concatenate([t1 * cos - t2 * sin, t1 * sin + t2 * cos], axis=-1)


def _ret_kernel(x_ref, cos_ref, sin_ref, win_ref, wout_ref, dmat_ref, qd_ref, kd_ref,
                g_ref, b_ref, o_ref, state_ref):
    @pl.when(pl.program_id(1) == 0)
    def _():
        state_ref[...] = jnp.zeros_like(state_ref)

    H, DK, DV = RET_HEADS, RET_DK, RET_DV
    half = DK // 2
    x = x_ref[...]
    tc = x.shape[0]
    proj = _dot(x.astype(BF16), win_ref[...])
    cos = cos_ref[...]
    sin = sin_ref[...]
    y = jnp.zeros_like(x)
    for h in range(H):
        q0 = h * DK
        k0 = H * DK + h * DK
        v0 = 2 * H * DK + h * DV
        g0 = 2 * H * DK + H * DV + h * DV
        qh = _rotate(proj[:, q0:q0 + half], proj[:, q0 + half:q0 + DK], cos, sin)
        kh = _rotate(proj[:, k0:k0 + half], proj[:, k0 + half:k0 + DK], cos, sin) * (DK ** -0.5)
        vh = proj[:, v0:v0 + DV].astype(BF16)
        gate = proj[:, g0:g0 + DV]
        s = lax.dot_general(qh.astype(BF16), kh.astype(BF16), (((1,), (1,)), ((), ())),
                            preferred_element_type=F32) * dmat_ref[h]
        state = state_ref[h]
        o = _dot(s.astype(BF16), vh) + _dot_bf(qh * qd_ref[h], state)
        kdec = (kh * kd_ref[h]).T.astype(BF16)
        chunk_decay = math.exp(tc * math.log1p(-(2.0 ** (-5.0 - h))))
        state_ref[h] = state * chunk_decay + _dot(kdec, vh)
        mu = jnp.mean(o, axis=-1, keepdims=True)
        d = o - mu
        var = jnp.mean(d * d, axis=-1, keepdims=True)
        o = d * lax.rsqrt(var + RET_GN_EPS)
        o = gate * jax.nn.sigmoid(gate) * o
        y = y + _dot(o.astype(BF16), wout_ref[h * DV:(h + 1) * DV, :])
    o_ref[...] = _layer_norm(ALPHA * x + y, g_ref[...], b_ref[...])


def _retention_ln(x, batch, w_in, w_out, g, b, tc=256):
    m = x.shape[0]
    seq = m // batch
    tc = min(tc, seq)
    nt = seq // tc
    H, DK = RET_HEADS, RET_DK
    half = DK // 2
    inv = ROPE_BASE ** (-jnp.arange(half, dtype=F32) / half)
    ang = jnp.arange(seq, dtype=F32)[:, None] * inv[None, :]
    cos, sin = jnp.cos(ang), jnp.sin(ang)
    log_gamma = jnp.log1p(-jnp.power(2.0, -5.0 - jnp.arange(H, dtype=F32)))
    pos = jnp.arange(tc, dtype=F32)
    rel = pos[:, None] - pos[None, :]
    dmat = jnp.where(rel >= 0, jnp.exp(jnp.maximum(rel, 0.0)[None] * log_gamma[:, None, None]), 0.0)
    qd = jnp.broadcast_to(jnp.exp((pos + 1.0)[None, :, None] * log_gamma[:, None, None]), (H, tc, DK))
    kd = jnp.broadcast_to(jnp.exp((tc - 1.0 - pos)[None, :, None] * log_gamma[:, None, None]), (H, tc, DK))

    tok = pl.BlockSpec((tc, D_MODEL), lambda bi, ti: (bi * nt + ti, 0))
    rope = pl.BlockSpec((tc, half), lambda bi, ti: (ti, 0))
    return pl.pallas_call(
        _ret_kernel,
        grid=(batch, nt),
        in_specs=[tok, rope, rope, _const_spec(w_in.shape), _const_spec(w_out.shape),
                  _const_spec(dmat.shape), _const_spec(qd.shape), _const_spec(kd.shape),
                  _const_spec(g.shape), _const_spec(b.shape)],
        out_specs=tok,
        out_shape=jax.ShapeDtypeStruct((m, D_MODEL), F32),
        scratch_shapes=[pltpu.VMEM((H, DK, RET_DV), F32)],
        compiler_params=_cparams("parallel", "arbitrary"),
        name="retention_ln",
    )(x, cos, sin, w_in, w_out, dmat, qd, kd, g, b)


def _s5_kernel(x_ref, perm_ref, permt_ref, win_ref, bre_ref, bim_ref, cre_ref, cim_ref, pwr_ref, pwi_ref, d_ref,
               wglu_ref, wout_ref, g_ref, b_ref, o_ref, hre_ref, him_ref, ere_ref, eim_ref,
               sre_ref, sim_ref, y_ref):
    @pl.when(pl.program_id(1) == 0)
    def _():
        sre_ref[...] = jnp.zeros_like(sre_ref)
        sim_ref[...] = jnp.zeros_like(sim_ref)

    L = S5_SUB
    tm = x_ref.shape[0]
    nr = tm // L
    cols = S5_GB * S5_GROUP
    x = x_ref[...]
    xp = _dot(perm_ref[...], x.astype(BF16)).astype(BF16)
    u = _dot(xp, win_ref[...])
    ub = u.astype(BF16)
    for gb in range(S5_NGB):
        ug = ub[:, gb * cols:(gb + 1) * cols]
        hre_ref[...] = _dot(ug, bre_ref[gb])
        him_ref[...] = _dot(ug, bim_ref[gb])
        sub = lambda j: slice(j * nr, (j + 1) * nr)
        ar = pwr_ref[gb, 1:2, :]
        ai = pwi_ref[gb, 1:2, :]
        hr = hre_ref[sub(0), :]
        hi = him_ref[sub(0), :]
        for j in range(1, L):
            hr, hi = (ar * hr - ai * hi + hre_ref[sub(j), :], ar * hi + ai * hr + him_ref[sub(j), :])
            hre_ref[sub(j), :] = hr
            him_ref[sub(j), :] = hi
        ere_ref[...] = hr
        eim_ref[...] = hi
        alr = pwr_ref[gb, L:L + 1, :]
        ali = pwi_ref[gb, L:L + 1, :]

        def row(c, carry):
            sr, si = carry
            er = ere_ref[pl.ds(c, 1), :]
            ei = eim_ref[pl.ds(c, 1), :]
            ere_ref[pl.ds(c, 1), :] = sr
            eim_ref[pl.ds(c, 1), :] = si
            return alr * sr - ali * si + er, alr * si + ali * sr + ei

        sr, si = lax.fori_loop(0, nr, row, (sre_ref[gb], sim_ref[gb]))
        sre_ref[gb] = sr
        sim_ref[gb] = si
        pr_in = ere_ref[...]
        pi_in = eim_ref[...]
        for j in range(L):
            pr = pwr_ref[gb, j + 1:j + 2, :]
            pi = pwi_ref[gb, j + 1:j + 2, :]
            hre_ref[sub(j), :] = hre_ref[sub(j), :] + (pr * pr_in - pi * pi_in)
            him_ref[sub(j), :] = him_ref[sub(j), :] + (pr * pi_in + pi * pr_in)
        y_ref[:, gb * cols:(gb + 1) * cols] = (_dot_bf(hre_ref[...], cre_ref[gb])
                                               + _dot_bf(him_ref[...], cim_ref[gb]))
    yv = y_ref[...] + d_ref[...] * u
    act = jax.nn.gelu(yv)
    z = act * jax.nn.sigmoid(_dot(act.astype(BF16), wglu_ref[...]))
    out = _dot(z.astype(BF16), wout_ref[...])
    hi16 = out.astype(BF16)
    lo16 = (out - hi16.astype(F32)).astype(BF16)
    out = _dot(permt_ref[...], hi16) + _dot(permt_ref[...], lo16)
    o_ref[...] = _layer_norm(ALPHA * x + out, g_ref[...], b_ref[...])


def _s5_operators(a_re, a_im, b_re, b_im, c_re, c_im, log_step):
    L = S5_SUB
    G, N, P = b_re.shape
    dt = jnp.exp(log_step)[:, None]
    mag = jnp.exp(dt * a_re)
    abar_re = mag * jnp.cos(dt * a_im)
    abar_im = mag * jnp.sin(dt * a_im)
    den = a_re * a_re + a_im * a_im
    f_re = ((abar_re - 1.0) * a_re + abar_im * a_im) / den
    f_im = (abar_im * a_re - (abar_re - 1.0) * a_im) / den
    bb_re = f_re[..., None] * b_re - f_im[..., None] * b_im
    bb_im = f_re[..., None] * b_im + f_im[..., None] * b_re
    pr, pi = [jnp.ones_like(abar_re)], [jnp.zeros_like(abar_re)]
    for _ in range(L):
        pr.append(pr[-1] * abar_re - pi[-1] * abar_im)
        pi.append(pr[-2] * abar_im + pi[-1] * abar_re)
    gbn, ngb = S5_GB, S5_NGB
    eye = jnp.eye(gbn, dtype=F32)
    blockdiag_in = lambda t: jnp.einsum('bgnp,gh->bgphn', t.reshape(ngb, gbn, N, P), eye).reshape(
        ngb, gbn * P, gbn * N)
    blockdiag_out = lambda t: jnp.einsum('bgpn,gh->bgnhp', t.reshape(ngb, gbn, P, N), eye).reshape(
        ngb, gbn * N, gbn * P)
    planes = lambda t: jnp.stack(t, 0).reshape(L + 1, ngb, gbn * N).transpose(1, 0, 2)
    return (blockdiag_in(bb_re).astype(BF16), blockdiag_in(bb_im).astype(BF16),
            blockdiag_out(c_re).astype(BF16), blockdiag_out(-c_im).astype(BF16), planes(pr), planes(pi))


def _s5_ln(x, batch, w_in, ops, d_skip, w_glu, w_out, g, b, tm=512):
    m = x.shape[0]
    seq = m // batch
    tm = min(tm, seq)
    nt = seq // tm
    nr = tm // S5_SUB
    rows = jnp.arange(tm)
    perm = (rows[None, :] == (rows[:, None] % nr) * S5_SUB + rows[:, None] // nr).astype(BF16)
    ws = [perm, perm.T, w_in, *ops, d_skip, w_glu, w_out, g, b]
    tok = pl.BlockSpec((tm, D_MODEL), lambda bi, ti: (bi * nt + ti, 0))
    wide = S5_GB * S5_STATE
    return pl.pallas_call(
        _s5_kernel,
        grid=(batch, nt),
        in_specs=[tok] + [_const_spec(w.shape) for w in ws],
        out_specs=tok,
        out_shape=jax.ShapeDtypeStruct((m, D_MODEL), F32),
        scratch_shapes=[pltpu.VMEM((tm, wide), F32), pltpu.VMEM((tm, wide), F32),
                        pltpu.VMEM((nr, wide), F32), pltpu.VMEM((nr, wide), F32),
                        pltpu.VMEM((S5_NGB, 1, wide), F32), pltpu.VMEM((S5_NGB, 1, wide), F32),
                        pltpu.VMEM((tm, D_MODEL), F32)],
        compiler_params=_cparams("parallel", "arbitrary"),
        name="s5_ln",
    )(x, *ws)


def _rw_proj_kernel(x_ref, xp_ref, mu_ref, wr_ref, wk_ref, wv_ref, w0_ref, w1_ref, w2_ref,
                    a0_ref, a1_ref, a2_ref, g1_ref, g2_ref,
                    r_ref, k_ref, v_ref, lw_ref, a_ref, g_ref, *, tiles_per_seq):
    x = x_ref[...]
    first = (pl.program_id(0) % tiles_per_seq) == 0
    prev = jnp.where(first, 0.0, xp_ref[7:8, :])
    rows = lax.broadcasted_iota(jnp.int32, x.shape, 0)
    shifted = jnp.where(rows == 0, prev, pltpu.roll(x, 1, 0))
    xx = shifted - x
    mix = lambda i: (x + xx * mu_ref[i:i + 1, :]).astype(BF16)
    r_ref[...] = _dot(mix(0), wr_ref[...])
    z = w0_ref[...] + _dot_bf(jnp.tanh(_dot(mix(1), w1_ref[...])), w2_ref[...])
    w = -_softplus(-z) - 0.5
    lw_ref[...] = -jnp.exp(w)
    k_ref[...] = _dot(mix(2), wk_ref[...])
    v_ref[...] = _dot(mix(3), wv_ref[...])
    a_ref[...] = jax.nn.sigmoid(a0_ref[...] + _dot_bf(_dot(mix(4), a1_ref[...]), a2_ref[...]))
    g_ref[...] = _dot_bf(jax.nn.sigmoid(_dot(mix(5), g1_ref[...])), g2_ref[...])


def _rw_scan_kernel(r_ref, k_ref, v_ref, lw_ref, a_ref, kk_ref, ka_ref, rk_ref, lg_ref, lb_ref,
                    mask_ref, ones_ref, y_ref, state_ref):
    @pl.when(pl.program_id(2) == 0)
    def _():
        state_ref[...] = jnp.zeros_like(state_ref)

    L, N = RW_CHUNK, RW_HEAD
    tm, width = r_ref.shape
    nc, nh = tm // L, width // N
    nt = (((1,), (1,)), ((), ()))
    tn = (((0,), (0,)), ((), ()))
    strict = mask_ref[0]
    incl = mask_ref[1]
    rr = lax.broadcasted_iota(jnp.int32, (L, L), 0)
    cc = lax.broadcasted_iota(jnp.int32, (L, L), 1)
    eye = jnp.where(rr == cc, 1.0, 0.0).astype(F32)

    lw = lw_ref[...]
    pos = lax.broadcasted_iota(jnp.int32, lw.shape, 0) & (L - 1)
    cum = lw
    step = 1
    while step < L:
        cum = cum + jnp.where(pos >= step, pltpu.roll(cum, step, 0), 0.0)
        step *= 2
    g_in = jnp.exp(cum)
    g_ex = jnp.exp(cum - lw)
    g_inv = jnp.exp(-cum)
    r = r_ref[...]
    k = k_ref[...]
    a = a_ref[...]
    def head_sum(t):
        hi = t.astype(BF16)
        lo = (t - hi.astype(F32)).astype(BF16)
        return _dot(hi, ones_ref[...]) + _dot(lo, ones_ref[...])

    kkr = k * kk_ref[...]
    kk_all = kkr / jnp.maximum(jnp.sqrt(head_sum(kkr * kkr)), 1e-12)
    k = k * (1.0 + (a - 1.0) * ka_ref[...])
    bonus = head_sum(r * k * rk_ref[...]) * v_ref[...]
    at_all = (-kk_all * g_ex).astype(BF16)
    bt_all = kk_all * a * g_inv
    rt_all = r * g_in
    kt_all = k * g_inv

    chains = [(hl, c) for hl in range(nh) for c in range(nc)]
    atc, rtc, ktc, btc, vc, gl = {}, {}, {}, {}, {}, {}
    for hl in range(nh):
        lanes = slice(hl * N, (hl + 1) * N)
        rt = rt_all[:, lanes].astype(BF16)
        at = at_all[:, lanes]
        kt = kt_all[:, lanes]
        bt = bt_all[:, lanes]
        vb = v_ref[:, lanes].astype(BF16)
        for c in range(nc):
            rows = slice(c * L, (c + 1) * L)
            atc[hl, c], rtc[hl, c], ktc[hl, c], btc[hl, c], vc[hl, c] = (
                at[rows], rt[rows], kt[rows], bt[rows], vb[rows])
            gl[hl, c] = g_in[(c + 1) * L - 1:(c + 1) * L, lanes]
    ar = {i: jnp.concatenate([atc[i], rtc[i]], axis=0) for i in chains}
    gab = {i: lax.dot_general(ar[i], btc[i].astype(BF16), nt, preferred_element_type=F32) for i in chains}
    gak = {i: lax.dot_general(ar[i], ktc[i].astype(BF16), nt, preferred_element_type=F32) for i in chains}
    a_ab = {i: gab[i][:L] * strict for i in chains}
    b_rb = {i: (gab[i][L:] * incl).astype(BF16) for i in chains}
    a_ak = {i: (gak[i][:L] * strict).astype(BF16) for i in chains}
    b_rk = {i: (gak[i][L:] * incl).astype(BF16) for i in chains}
    tinv = {i: eye + a_ab[i] * mask_ref[2] for i in chains}
    lvl, bsz = 1, 2
    while bsz < L:
        mask = mask_ref[2 + lvl]
        half = {i: _dot_bf(a_ab[i] * mask, tinv[i]) for i in chains}
        tinv = {i: tinv[i] + _dot_bf(tinv[i], half[i]) for i in chains}
        bsz *= 2
        lvl += 1
    tb = {i: tinv[i].astype(BF16) for i in chains}
    ap = {i: _dot(tb[i], atc[i]).astype(BF16) for i in chains}
    akv = {i: _dot(a_ak[i], vc[i]).astype(BF16) for i in chains}
    wloc = {i: _dot(tb[i], akv[i]) for i in chains}
    bv = {i: _dot(b_rk[i], vc[i]) for i in chains}
    bg = {i: (btc[i] * gl[i]).astype(BF16) for i in chains}
    kv = {i: lax.dot_general(vc[i], (ktc[i] * gl[i]).astype(BF16), tn, preferred_element_type=F32)
          for i in chains}
    apr = {i: jnp.concatenate([ap[i], rtc[i]], axis=0) for i in chains}

    states = [state_ref[hl] for hl in range(nh)]
    outs = [[] for _ in range(nh)]
    for c in range(nc):
        both = [lax.dot_general(apr[hl, c], states[hl].astype(BF16), nt, preferred_element_type=F32)
                for hl in range(nh)]
        ub = [(both[hl][:L] + wloc[hl, c]).astype(BF16) for hl in range(nh)]
        for hl in range(nh):
            states[hl] = (states[hl] * gl[hl, c] + kv[hl, c]
                          + lax.dot_general(ub[hl], bg[hl, c], tn, preferred_element_type=F32))
        for hl in range(nh):
            outs[hl].append(both[hl][L:] + _dot(b_rb[hl, c], ub[hl]) + bv[hl, c])

    for hl in range(nh):
        lanes = slice(hl * N, (hl + 1) * N)
        state_ref[hl] = states[hl]
        o = jnp.concatenate(outs[hl], axis=0)
        mu = jnp.mean(o, axis=-1, keepdims=True)
        d = o - mu
        var = jnp.mean(d * d, axis=-1, keepdims=True)
        o = d * lax.rsqrt(var + RW_GN_EPS) * lg_ref[:, lanes] + lb_ref[:, lanes]
        o = o + bonus[:, lanes]
        y_ref[:, lanes] = o


def _rw_out_kernel(x_ref, y_ref, gate_ref, wo_ref, g_ref, b_ref, o_ref):
    out = _dot((y_ref[...] * gate_ref[...]).astype(BF16), wo_ref[...])
    o_ref[...] = _layer_norm(ALPHA * x_ref[...] + out, g_ref[...], b_ref[...])


def _rw_masks():
    L = RW_CHUNK
    i = jnp.arange(L)
    r, c = i[:, None], i[None, :]
    masks = [r > c, r >= c]
    bsz = 1
    while bsz < L:
        masks.append((r // (2 * bsz) == c // (2 * bsz)) & ((r // bsz) % 2 == 1) & ((c // bsz) % 2 == 0))
        bsz *= 2
    return jnp.stack(masks, 0).astype(F32)


def _rwkv_ln(x, batch, p, g, b, tm=512):
    m = x.shape[0]
    seq = m // batch
    tm = min(tm, seq)
    nt = seq // tm
    tok = pl.BlockSpec((tm, D_MODEL), lambda i: (i, 0))
    prev = pl.BlockSpec((8, D_MODEL), lambda i: (jnp.maximum(i * (tm // 8) - 1, 0), 0))
    names = ['mu', 'w_r', 'w_k', 'w_v', 'w0', 'w1', 'w2', 'a0', 'a1', 'a2', 'g1', 'g2']
    ws = [p[n] for n in names]
    act = jax.ShapeDtypeStruct((m, D_MODEL), F32)
    r, k, v, lw, a, gate = pl.pallas_call(
        functools.partial(_rw_proj_kernel, tiles_per_seq=nt),
        grid=(m // tm,),
        in_specs=[tok, prev] + [_const_spec(w.shape) for w in ws],
        out_specs=[tok] * 6,
        out_shape=[act] * 6,
        compiler_params=_cparams("parallel"),
        name="rwkv_proj",
    )(x, x, *ws)

    lanes = RW_HB * RW_HEAD
    ts = min(RW_SCAN_TILE, seq)
    nts = seq // ts
    hblk = pl.BlockSpec((ts, lanes), lambda bi, hi, ti: (bi * nts + ti, hi))
    pblk = pl.BlockSpec((1, lanes), lambda bi, hi, ti: (0, hi))
    masks = _rw_masks()
    head_of_lane = jnp.arange(lanes) // RW_HEAD
    ones = (head_of_lane[:, None] == head_of_lane[None, :]).astype(BF16)
    y = pl.pallas_call(
        _rw_scan_kernel,
        grid=(batch, RW_HEADS // RW_HB, nts),
        in_specs=[hblk] * 5 + [pblk] * 5 + [_const_spec(masks.shape), _const_spec(ones.shape)],
        out_specs=hblk,
        out_shape=act,
        scratch_shapes=[pltpu.VMEM((RW_HB, RW_HEAD, RW_HEAD), F32)],
        compiler_params=_cparams("parallel", "parallel", "arbitrary"),
        name="rwkv_scan",
    )(r, k, v, lw, a, p['k_k'], p['k_a'], p['r_k'], p['lnx_g'], p['lnx_b'], masks, ones)

    return pl.pallas_call(
        _rw_out_kernel,
        grid=(m // tm,),
        in_specs=[tok, tok, tok, _const_spec(p['w_o'].shape), _const_spec(g.shape), _const_spec(b.shape)],
        out_specs=tok,
        out_shape=act,
        compiler_params=_cparams("parallel"),
        name="rwkv_out_ln",
    )(x, y, gate, p['w_o'], g, b)


def _lru_kernel(x_ref, win_ref, cw_ref, cb_ref, wa_ref, ba_ref, wx_ref, bx_ref, lam_ref, wout_ref,
                g_ref, b_ref, o_ref, xr_ref, a_ref, inp_ref, h_ref):
    W = D_MODEL
    tm = x_ref.shape[0]

    @pl.when(pl.program_id(1) == 0)
    def _():
        xr_ref[...] = jnp.zeros_like(xr_ref)
        h_ref[...] = jnp.zeros_like(h_ref)

    x = x_ref[...]
    proj = _dot(x.astype(BF16), win_ref[...])
    gate = jax.nn.gelu(proj[:, :W])
    xr_ref[0:8, :] = xr_ref[tm:tm + 8, :]
    xr_ref[8:tm + 8, :] = proj[:, W:]
    xc = cb_ref[...] + cw_ref[3:4, :] * xr_ref[8:tm + 8, :]
    for j in range(CONV_WIDTH - 1):
        xc = xc + cw_ref[j:j + 1, :] * xr_ref[pl.ds(5 + j, tm), :]
    xcb = xc.astype(BF16)
    gr = jnp.concatenate([_dot(xcb[:, kb * LRU_BLOCK:(kb + 1) * LRU_BLOCK], wa_ref[kb])
                          for kb in range(LRU_BLOCKS)], axis=-1) + ba_ref[...]
    gi = jnp.concatenate([_dot(xcb[:, kb * LRU_BLOCK:(kb + 1) * LRU_BLOCK], wx_ref[kb])
                          for kb in range(LRU_BLOCKS)], axis=-1) + bx_ref[...]
    log_a = -LRU_C * jax.nn.sigmoid(gr) * _softplus(-lam_ref[...])
    av = jnp.exp(log_a)
    a_ref[...] = av
    inp_ref[...] = jnp.sqrt(1.0 - av * av) * (jax.nn.sigmoid(gi) * xc)

    def block(i, h):
        rows = pl.ds(pl.multiple_of(i * 8, 8), 8)
        ab = a_ref[rows, :]
        bb = inp_ref[rows, :]
        out = []
        for j in range(8):
            h = ab[j:j + 1, :] * h + bb[j:j + 1, :]
            out.append(h)
        inp_ref[rows, :] = jnp.concatenate(out, axis=0)
        return h

    h_ref[...] = lax.fori_loop(0, tm // 8, block, h_ref[...])
    y = _dot((inp_ref[...] * gate).astype(BF16), wout_ref[...])
    o_ref[...] = _layer_norm(ALPHA * x + y, g_ref[...], b_ref[...])


def _lru_ln(x, batch, p, g, b, tm=256):
    m = x.shape[0]
    seq = m // batch
    tm = min(tm, seq)
    nt = seq // tm
    tok = pl.BlockSpec((tm, D_MODEL), lambda bi, ti: (bi * nt + ti, 0))
    names = ['w_in', 'conv_w', 'conv_b', 'w_a', 'b_a', 'w_x', 'b_x', 'lam', 'w_out']
    ws = [p[n] for n in names]
    return pl.pallas_call(
        _lru_kernel,
        grid=(batch, nt),
        in_specs=[tok] + [_const_spec(w.shape) for w in ws] + [_const_spec(g.shape), _const_spec(b.shape)],
        out_specs=tok,
        out_shape=jax.ShapeDtypeStruct((m, D_MODEL), F32),
        scratch_shapes=[pltpu.VMEM((tm + 8, D_MODEL), F32), pltpu.VMEM((tm, D_MODEL), F32),
                        pltpu.VMEM((tm, D_MODEL), F32), pltpu.VMEM((1, D_MODEL), F32)],
        compiler_params=_cparams("parallel", "arbitrary"),
        name="rglru_ln",
    )(x, *ws, g, b)


def kernel(x, ln_g, ln_b, ffn_w1, ffn_w3, ffn_w2, ret_w_in, ret_w_out, s5_w_in, s5_a_re, s5_a_im, s5_b_re, s5_b_im, s5_c_re, s5_c_im, s5_d, s5_log_step, s5_w_glu, s5_w_out, rw_mu, rw_w_r, rw_w_k, rw_w_v, rw_w0, rw_w1, rw_w2, rw_a0, rw_a1, rw_a2, rw_g1, rw_g2, rw_k_k, rw_k_a, rw_r_k, rw_lnx_g, rw_lnx_b, rw_w_o, lru_w_in, lru_conv_w, lru_conv_b, lru_w_a, lru_b_a, lru_w_x, lru_b_x, lru_lambda, lru_w_out):
    batch, seq, d = x.shape
    depth = ln_g.shape[0]
    h = x.reshape(batch * seq, d)
    bf = lambda t: t.astype(BF16)
    row = lambda t: t.reshape(1, -1)
    for i in range(depth):
        m, j = i % N_MIXERS, i // N_MIXERS
        h = _ffn_ln(h, bf(ffn_w1[i, 0]), bf(ffn_w3[i, 0]), bf(ffn_w2[i, 0]), row(ln_g[i, 0]), row(ln_b[i, 0]))
        g, b = row(ln_g[i, 1]), row(ln_b[i, 1])
        if m == 0:
            h = _retention_ln(h, batch, bf(ret_w_in[j]), bf(ret_w_out[j]), g, b)
        elif m == 1:
            ops = _s5_operators(s5_a_re[j], s5_a_im[j], s5_b_re[j], s5_b_im[j], s5_c_re[j], s5_c_im[j],
                                s5_log_step[j])
            h = _s5_ln(h, batch, bf(s5_w_in[j]), ops, row(s5_d[j]), bf(s5_w_glu[j]), bf(s5_w_out[j]), g, b)
        elif m == 2:
            p = dict(mu=rw_mu[j], w_r=bf(rw_w_r[j]), w_k=bf(rw_w_k[j]), w_v=bf(rw_w_v[j]),
                     w0=row(rw_w0[j]), w1=bf(rw_w1[j]), w2=bf(rw_w2[j]),
                     a0=row(rw_a0[j]), a1=bf(rw_a1[j]), a2=bf(rw_a2[j]), g1=bf(rw_g1[j]), g2=bf(rw_g2[j]),
                     k_k=row(rw_k_k[j]), k_a=row(rw_k_a[j]), r_k=row(rw_r_k[j]),
                     lnx_g=row(rw_lnx_g[j]), lnx_b=row(rw_lnx_b[j]), w_o=bf(rw_w_o[j]))
            h = _rwkv_ln(h, batch, p, g, b)
        else:
            p = dict(w_in=bf(lru_w_in[j]), conv_w=lru_conv_w[j], conv_b=row(lru_conv_b[j]),
                     w_a=bf(lru_w_a[j]), b_a=row(lru_b_a[j]), w_x=bf(lru_w_x[j]), b_x=row(lru_b_x[j]),
                     lam=row(lru_lambda[j]), w_out=bf(lru_w_out[j]))
            h = _lru_ln(h, batch, p, g, b)
        h = _ffn_ln(h, bf(ffn_w1[i, 1]), bf(ffn_w3[i, 1]), bf(ffn_w2[i, 1]), row(ln_g[i, 2]), row(ln_b[i, 2]))
    return h.reshape(batch, seq, d)
```

```python
import functools
import math

import jax
import jax.numpy as jnp
from jax import lax
from jax.experimental import pallas as pl
from jax.experimental.pallas import tpu as pltpu

F32 = jnp.float32
BF16 = jnp.bfloat16

D_MODEL = 1024
DEPTH = 4
N_MIXERS = 4
ALPHA = (2 * DEPTH) ** 0.25
LN_EPS = 1e-5
D_FF = 2816
FFN_HALF = 0.5

RET_HEADS = 4
RET_DK = 256
RET_DV = 512
ROPE_BASE = 10000.0
RET_GN_EPS = 1e-5

S5_GROUPS = 64
S5_GROUP = 16
S5_STATE = 64
S5_SUB = 16
S5_GB = 16
S5_NGB = S5_GROUPS // S5_GB

RW_HEADS = 16
RW_HEAD = 64
RW_GN_EPS = 64e-5
RW_CHUNK = 64
RW_HB = 16
RW_SUM_LANES = 256
RW_SCAN_TILE = 256

LRU_BLOCKS = 4
LRU_BLOCK = 256
CONV_WIDTH = 4
LRU_C = 8.0

VMEM_LIMIT = 56 * 1024 * 1024


def _cparams(*sem):
    return pltpu.CompilerParams(dimension_semantics=sem, vmem_limit_bytes=VMEM_LIMIT)


def _const_spec(shape):
    nd = len(shape)
    return pl.BlockSpec(shape, lambda *_: (0,) * nd, pipeline_mode=pl.Buffered(1))


def _layer_norm(r, g, b):
    mu = jnp.mean(r, axis=-1, keepdims=True)
    d = r - mu
    var = jnp.mean(d * d, axis=-1, keepdims=True)
    return d * lax.rsqrt(var + LN_EPS) * g + b


def _dot(a, b):
    return jnp.dot(a, b, preferred_element_type=F32)


def _dot_bf(a, b):
    return jnp.dot(a.astype(BF16), b.astype(BF16), preferred_element_type=F32)


def _softplus(z):
    return jnp.maximum(z, 0.0) + jnp.log(1.0 + jnp.exp(-jnp.abs(z)))


def _ffn_kernel(x_ref, w1_ref, w3_ref, w2_ref, g_ref, b_ref, o_ref):
    x = x_ref[...]
    xb = x.astype(BF16)
    h1 = _dot(xb, w1_ref[...])
    h3 = _dot(xb, w3_ref[...])
    act = (h1 * jax.nn.sigmoid(h1) * h3).astype(BF16)
    y = _dot(act, w2_ref[...])
    o_ref[...] = _layer_norm(ALPHA * x + FFN_HALF * y, g_ref[...], b_ref[...])


def _ffn_ln(x, w1, w3, w2, g, b, tm=512):
    m = x.shape[0]
    tm = min(tm, m)
    tok = pl.BlockSpec((tm, D_MODEL), lambda i: (i, 0))
    return pl.pallas_call(
        _ffn_kernel,
        grid=(m // tm,),
        in_specs=[tok, _const_spec(w1.shape), _const_spec(w3.shape), _const_spec(w2.shape),
                  _const_spec(g.shape), _const_spec(b.shape)],
        out_specs=tok,
        out_shape=jax.ShapeDtypeStruct((m, D_MODEL), F32),
        compiler_params=_cparams("parallel"),
        name="ffn_ln",
    )(x, w1, w3, w2, g, b)


def _rotate(t1, t2, cos, sin):
    return jnp.concatenate([t1 * cos - t2 * sin, t1 * sin + t2 * cos], axis=-1)


def _ret_kernel(x_ref, cos_ref, sin_ref, win_ref, wout_ref, dmat_ref, qd_ref, kd_ref,
                g_ref, b_ref, o_ref, state_ref):
    @pl.when(pl.program_id(1) == 0)
    def _():
        state_ref[...] = jnp.zeros_like(state_ref)

    H, DK, DV = RET_HEADS, RET_DK, RET_DV
    half = DK // 2
    x = x_ref[...]
    tc = x.shape[0]
    proj = _dot(x.astype(BF16), win_ref[...])
    cos = cos_ref[...]
    sin = sin_ref[...]
    y = jnp.zeros_like(x)
    for h in range(H):
        q0 = h * DK
        k0 = H * DK + h * DK
        v0 = 2 * H * DK + h * DV
        g0 = 2 * H * DK + H * DV + h * DV
        qh = _rotate(proj[:, q0:q0 + half], proj[:, q0 + half:q0 + DK], cos, sin)
        kh = _rotate(proj[:, k0:k0 + half], proj[:, k0 + half:k0 + DK], cos, sin) * (DK ** -0.5)
        vh = proj[:, v0:v0 + DV].astype(BF16)
        gate = proj[:, g0:g0 + DV]
        s = lax.dot_general(qh.astype(BF16), kh.astype(BF16), (((1,), (1,)), ((), ())),
                            preferred_element_type=F32) * dmat_ref[h]
        state = state_ref[h]
        o = _dot(s.astype(BF16), vh) + _dot_bf(qh * qd_ref[h], state)
        kdec = (kh * kd_ref[h]).T.astype(BF16)
        chunk_decay = math.exp(tc * math.log1p(-(2.0 ** (-5.0 - h))))
        state_ref[h] = state * chunk_decay + _dot(kdec, vh)
        mu = jnp.mean(o, axis=-1, keepdims=True)
        d = o - mu
        var = jnp.mean(d * d, axis=-1, keepdims=True)
        o = d * lax.rsqrt(var + RET_GN_EPS)
        o = gate * jax.nn.sigmoid(gate) * o
        y = y + _dot(o.astype(BF16), wout_ref[h * DV:(h + 1) * DV, :])
    o_ref[...] = _layer_norm(ALPHA * x + y, g_ref[...], b_ref[...])


def _retention_ln(x, batch, w_in, w_out, g, b, tc=256):
    m = x.shape[0]
    seq = m // batch
    tc = min(tc, seq)
    nt = seq // tc
    H, DK = RET_HEADS, RET_DK
    half = DK // 2
    inv = ROPE_BASE ** (-jnp.arange(half, dtype=F32) / half)
    ang = jnp.arange(seq, dtype=F32)[:, None] * inv[None, :]
    cos, sin = jnp.cos(ang), jnp.sin(ang)
    log_gamma = jnp.log1p(-jnp.power(2.0, -5.0 - jnp.arange(H, dtype=F32)))
    pos = jnp.arange(tc, dtype=F32)
    rel = pos[:, None] - pos[None, :]
    dmat = jnp.where(rel >= 0, jnp.exp(jnp.maximum(rel, 0.0)[None] * log_gamma[:, None, None]), 0.0)
    qd = jnp.broadcast_to(jnp.exp((pos + 1.0)[None, :, None] * log_gamma[:, None, None]), (H, tc, DK))
    kd = jnp.broadcast_to(jnp.exp((tc - 1.0 - pos)[None, :, None] * log_gamma[:, None, None]), (H, tc, DK))

    tok = pl.BlockSpec((tc, D_MODEL), lambda bi, ti: (bi * nt + ti, 0))
    rope = pl.BlockSpec((tc, half), lambda bi, ti: (ti, 0))
    return pl.pallas_call(
        _ret_kernel,
        grid=(batch, nt),
        in_specs=[tok, rope, rope, _const_spec(w_in.shape), _const_spec(w_out.shape),
                  _const_spec(dmat.shape), _const_spec(qd.shape), _const_spec(kd.shape),
                  _const_spec(g.shape), _const_spec(b.shape)],
        out_specs=tok,
        out_shape=jax.ShapeDtypeStruct((m, D_MODEL), F32),
        scratch_shapes=[pltpu.VMEM((H, DK, RET_DV), F32)],
        compiler_params=_cparams("parallel", "arbitrary"),
        name="retention_ln",
    )(x, cos, sin, w_in, w_out, dmat, qd, kd, g, b)


def _s5_kernel(x_ref, perm_ref, permt_ref, win_ref, bre_ref, bim_ref, cre_ref, cim_ref, pwr_ref, pwi_ref, d_ref,
               wglu_ref, wout_ref, g_ref, b_ref, o_ref, hre_ref, him_ref, ere_ref, eim_ref,
               sre_ref, sim_ref, y_ref):
    @pl.when(pl.program_id(1) == 0)
    def _():
        sre_ref[...] = jnp.zeros_like(sre_ref)
        sim_ref[...] = jnp.zeros_like(sim_ref)

    L = S5_SUB
    tm = x_ref.shape[0]
    nr = tm // L
    cols = S5_GB * S5_GROUP
    x = x_ref[...]
    xp = _dot(perm_ref[...], x.astype(BF16)).astype(BF16)
    u = _dot(xp, win_ref[...])
    ub = u.astype(BF16)
    for gb in range(S5_NGB):
        ug = ub[:, gb * cols:(gb + 1) * cols]
        hre_ref[...] = _dot(ug, bre_ref[gb])
        him_ref[...] = _dot(ug, bim_ref[gb])
        sub = lambda j: slice(j * nr, (j + 1) * nr)
        ar = pwr_ref[gb, 1:2, :]
        ai = pwi_ref[gb, 1:2, :]
        hr = hre_ref[sub(0), :]
        hi = him_ref[sub(0), :]
        for j in range(1, L):
            hr, hi = (ar * hr - ai * hi + hre_ref[sub(j), :], ar * hi + ai * hr + him_ref[sub(j), :])
            hre_ref[sub(j), :] = hr
            him_ref[sub(j), :] = hi
        ere_ref[...] = hr
        eim_ref[...] = hi
        alr = pwr_ref[gb, L:L + 1, :]
        ali = pwi_ref[gb, L:L + 1, :]

        def row(c, carry):
            sr, si = carry
            er = ere_ref[pl.ds(c, 1), :]
            ei = eim_ref[pl.ds(c, 1), :]
            ere_ref[pl.ds(c, 1), :] = sr
            eim_ref[pl.ds(c, 1), :] = si
            return alr * sr - ali * si + er, alr * si + ali * sr + ei

        sr, si = lax.fori_loop(0, nr, row, (sre_ref[gb], sim_ref[gb]))
        sre_ref[gb] = sr
        sim_ref[gb] = si
        pr_in = ere_ref[...]
        pi_in = eim_ref[...]
        for j in range(L):
            pr = pwr_ref[gb, j + 1:j + 2, :]
            pi = pwi_ref[gb, j + 1:j + 2, :]
            hre_ref[sub(j), :] = hre_ref[sub(j), :] + (pr * pr_in - pi * pi_in)
            him_ref[sub(j), :] = him_ref[sub(j), :] + (pr * pi_in + pi * pr_in)
        y_ref[:, gb * cols:(gb + 1) * cols] = (_dot_bf(hre_ref[...], cre_ref[gb])
                                               + _dot_bf(him_ref[...], cim_ref[gb]))
    yv = y_ref[...] + d_ref[...] * u
    act = jax.nn.gelu(yv)
    z = act * jax.nn.sigmoid(_dot(act.astype(BF16), wglu_ref[...]))
    out = _dot(z.astype(BF16), wout_ref[...])
    hi16 = out.astype(BF16)
    lo16 = (out - hi16.astype(F32)).astype(BF16)
    out = _dot(permt_ref[...], hi16) + _dot(permt_ref[...], lo16)
    o_ref[...] = _layer_norm(ALPHA * x + out, g_ref[...], b_ref[...])


def _s5_operators(a_re, a_im, b_re, b_im, c_re, c_im, log_step):
    L = S5_SUB
    G, N, P = b_re.shape
    dt = jnp.exp(log_step)[:, None]
    mag = jnp.exp(dt * a_re)
    abar_re = mag * jnp.cos(dt * a_im)
    abar_im = mag * jnp.sin(dt * a_im)
    den = a_re * a_re + a_im * a_im
    f_re = ((abar_re - 1.0) * a_re + abar_im * a_im) / den
    f_im = (abar_im * a_re - (abar_re - 1.0) * a_im) / den
    bb_re = f_re[..., None] * b_re - f_im[..., None] * b_im
    bb_im = f_re[..., None] * b_im + f_im[..., None] * b_re
    pr, pi = [jnp.ones_like(abar_re)], [jnp.zeros_like(abar_re)]
    for _ in range(L):
        pr.append(pr[-1] * abar_re - pi[-1] * abar_im)
        pi.append(pr[-2] * abar_im + pi[-1] * abar_re)
    gbn, ngb = S5_GB, S5_NGB
    eye = jnp.eye(gbn, dtype=F32)
    blockdiag_in = lambda t: jnp.einsum('bgnp,gh->bgphn', t.reshape(ngb, gbn, N, P), eye).reshape(
        ngb, gbn * P, gbn * N)
    blockdiag_out = lambda t: jnp.einsum('bgpn,gh->bgnhp', t.reshape(ngb, gbn, P, N), eye).reshape(
        ngb, gbn * N, gbn * P)
    planes = lambda t: jnp.stack(t, 0).reshape(L + 1, ngb, gbn * N).transpose(1, 0, 2)
    return (blockdiag_in(bb_re).astype(BF16), blockdiag_in(bb_im).astype(BF16),
            blockdiag_out(c_re).astype(BF16), blockdiag_out(-c_im).astype(BF16), planes(pr), planes(pi))


def _s5_ln(x, batch, w_in, ops, d_skip, w_glu, w_out, g, b, tm=512):
    m = x.shape[0]
    seq = m // batch
    tm = min(tm, seq)
    nt = seq // tm
    nr = tm // S5_SUB
    rows = jnp.arange(tm)
    perm = (rows[None, :] == (rows[:, None] % nr) * S5_SUB + rows[:, None] // nr).astype(BF16)
    ws = [perm, perm.T, w_in, *ops, d_skip, w_glu, w_out, g, b]
    tok = pl.BlockSpec((tm, D_MODEL), lambda bi, ti: (bi * nt + ti, 0))
    wide = S5_GB * S5_STATE
    return pl.pallas_call(
        _s5_kernel,
        grid=(batch, nt),
        in_specs=[tok] + [_const_spec(w.shape) for w in ws],
        out_specs=tok,
        out_shape=jax.ShapeDtypeStruct((m, D_MODEL), F32),
        scratch_shapes=[pltpu.VMEM((tm, wide), F32), pltpu.VMEM((tm, wide), F32),
                        pltpu.VMEM((nr, wide), F32), pltpu.VMEM((nr, wide), F32),
                        pltpu.VMEM((S5_NGB, 1, wide), F32), pltpu.VMEM((S5_NGB, 1, wide), F32),
                        pltpu.VMEM((tm, D_MODEL), F32)],
        compiler_params=_cparams("parallel", "arbitrary"),
        name="s5_ln",
    )(x, *ws)


def _rw_proj_kernel(x_ref, xp_ref, mu_ref, wr_ref, wk_ref, wv_ref, w0_ref, w1_ref, w2_ref,
                    a0_ref, a1_ref, a2_ref, g1_ref, g2_ref,
                    r_ref, k_ref, v_ref, lw_ref, a_ref, g_ref, *, tiles_per_seq):
    x = x_ref[...]
    first = (pl.program_id(0) % tiles_per_seq) == 0
    prev = jnp.where(first, 0.0, xp_ref[7:8, :])
    rows = lax.broadcasted_iota(jnp.int32, x.shape, 0)
    shifted = jnp.where(rows == 0, prev, pltpu.roll(x, 1, 0))
    xx = shifted - x
    mix = lambda i: (x + xx * mu_ref[i:i + 1, :]).astype(BF16)
    r_ref[...] = _dot(mix(0), wr_ref[...])
    z = w0_ref[...] + _dot_bf(jnp.tanh(_dot(mix(1), w1_ref[...])), w2_ref[...])
    w = -_softplus(-z) - 0.5
    lw_ref[...] = -jnp.exp(w)
    k_ref[...] = _dot(mix(2), wk_ref[...])
    v_ref[...] = _dot(mix(3), wv_ref[...])
    a_ref[...] = jax.nn.sigmoid(a0_ref[...] + _dot_bf(_dot(mix(4), a1_ref[...]), a2_ref[...]))
    g_ref[...] = _dot_bf(jax.nn.sigmoid(_dot(mix(5), g1_ref[...])), g2_ref[...])


def _rw_scan_kernel(r_ref, k_ref, v_ref, lw_ref, a_ref, kk_ref, ka_ref, rk_ref, lg_ref, lb_ref,
                    mask_ref, ones_ref, bd_ref, y_ref, state_ref):
    @pl.when(pl.program_id(2) == 0)
    def _():
        state_ref[...] = jnp.zeros_like(state_ref)

    L, N = RW_CHUNK, RW_HEAD
    PW = 2 * N
    tm, width = r_ref.shape
    nc, npair = tm // L, width // PW
    nt = (((1,), (1,)), ((), ()))
    tn = (((0,), (0,)), ((), ()))
    strict = mask_ref[0]
    incl = mask_ref[1]
    rr = lax.broadcasted_iota(jnp.int32, (L, PW), 0)
    cc = lax.broadcasted_iota(jnp.int32, (L, PW), 1)
    eye = jnp.where(rr == (cc & (N - 1)), 1.0, 0.0).astype(F32)
    first = cc < N
    bd = bd_ref[...]

    def bdiag(t):
        zero = jnp.zeros_like(t)
        return jnp.concatenate([jnp.where(first, t, zero), jnp.where(first, zero, t)], axis=0)

    lw = lw_ref[...]
    pos = lax.broadcasted_iota(jnp.int32, lw.shape, 0) & (L - 1)
    cum = lw
    step = 1
    while step < L:
        cum = cum + jnp.where(pos >= step, pltpu.roll(cum, step, 0), 0.0)
        step *= 2
    g_in = jnp.exp(cum)
    g_ex = jnp.exp(cum - lw)
    g_inv = jnp.exp(-cum)
    r = r_ref[...]
    k = k_ref[...]
    a = a_ref[...]

    def head_sum(t):
        hw = ones_ref.shape[0]
        hi = t.astype(BF16)
        lo = (t - hi.astype(F32)).astype(BF16)
        return jnp.concatenate(
            [_dot(hi[:, j:j + hw], ones_ref[...]) + _dot(lo[:, j:j + hw], ones_ref[...])
             for j in range(0, t.shape[1], hw)], axis=-1)

    kkr = k * kk_ref[...]
    kk_all = kkr / jnp.maximum(jnp.sqrt(head_sum(kkr * kkr)), 1e-12)
    k = k * (1.0 + (a - 1.0) * ka_ref[...])
    bonus = head_sum(r * k * rk_ref[...]) * v_ref[...]
    at_all = (-kk_all * g_ex).astype(BF16)
    bt_all = kk_all * a * g_inv
    rt_all = (r * g_in).astype(BF16)
    kt_all = k * g_inv
    vb_all = v_ref[...].astype(BF16)

    chains = [(p, c) for p in range(npair) for c in range(nc)]
    sl = lambda t, i: t[i[1] * L:(i[1] + 1) * L, i[0] * PW:(i[0] + 1) * PW]
    atc = {i: sl(at_all, i) for i in chains}
    rtc = {i: sl(rt_all, i) for i in chains}
    ktc = {i: sl(kt_all, i) for i in chains}
    btc = {i: sl(bt_all, i) for i in chains}
    vbd = {i: bdiag(sl(vb_all, i)) for i in chains}
    gl = {i: g_in[(i[1] + 1) * L - 1:(i[1] + 1) * L, i[0] * PW:(i[0] + 1) * PW] for i in chains}
    ar = {i: jnp.concatenate([atc[i], rtc[i]], axis=0) for i in chains}
    gab = {i: lax.dot_general(ar[i], bdiag(btc[i].astype(BF16)), nt, preferred_element_type=F32)
           for i in chains}
    gak = {i: lax.dot_general(ar[i], bdiag(ktc[i].astype(BF16)), nt, preferred_element_type=F32)
           for i in chains}
    a_ab = {i: gab[i][:L] * strict for i in chains}
    b_rb = {i: (gab[i][L:] * incl).astype(BF16) for i in chains}
    a_ak = {i: (gak[i][:L] * strict).astype(BF16) for i in chains}
    b_rk = {i: (gak[i][L:] * incl).astype(BF16) for i in chains}
    tinv = {i: eye + a_ab[i] * mask_ref[2] for i in chains}
    lvl, bsz = 1, 2
    while bsz < L:
        mask = mask_ref[2 + lvl]
        half = {i: _dot((a_ab[i] * mask).astype(BF16), bdiag(tinv[i].astype(BF16))) for i in chains}
        tinv = {i: tinv[i] + _dot(tinv[i].astype(BF16), bdiag(half[i].astype(BF16))) for i in chains}
        bsz *= 2
        lvl += 1
    tb = {i: tinv[i].astype(BF16) for i in chains}
    ap = {i: _dot(tb[i], bdiag(atc[i])).astype(BF16) for i in chains}
    akv = {i: _dot(a_ak[i], vbd[i]).astype(BF16) for i in chains}
    wloc = {i: _dot(tb[i], bdiag(akv[i])) for i in chains}
    bv = {i: _dot(b_rk[i], vbd[i]) for i in chains}
    bg = {i: (btc[i] * gl[i]).astype(BF16) for i in chains}
    kv = {i: lax.dot_general(sl(vb_all, i), (ktc[i] * gl[i]).astype(BF16), tn,
                             preferred_element_type=F32) * bd for i in chains}
    apr = {i: jnp.concatenate([ap[i], rtc[i]], axis=0) for i in chains}

    states = [state_ref[p] for p in range(npair)]
    outs = [[] for _ in range(npair)]
    for c in range(nc):
        both = [lax.dot_general(apr[p, c], states[p].astype(BF16), nt, preferred_element_type=F32)
                for p in range(npair)]
        ub = [(both[p][:L] + wloc[p, c]).astype(BF16) for p in range(npair)]
        for p in range(npair):
            states[p] = (states[p] * gl[p, c] + kv[p, c]
                         + lax.dot_general(ub[p], bg[p, c], tn, preferred_element_type=F32) * bd)
        for p in range(npair):
            outs[p].append(both[p][L:] + _dot(b_rb[p, c], bdiag(ub[p])) + bv[p, c])

    for p in range(npair):
        state_ref[p] = states[p]
    o = jnp.concatenate([jnp.concatenate(outs[p], axis=0) for p in range(npair)], axis=-1)
    mu = head_sum(o) * (1.0 / N)
    d = o - mu
    var = head_sum(d * d) * (1.0 / N)
    y_ref[...] = d * lax.rsqrt(var + RW_GN_EPS) * lg_ref[...] + lb_ref[...] + bonus


def _rw_out_kernel(x_ref, y_ref, gate_ref, wo_ref, g_ref, b_ref, o_ref):
    out = _dot((y_ref[...] * gate_ref[...]).astype(BF16), wo_ref[...])
    o_ref[...] = _layer_norm(ALPHA * x_ref[...] + out, g_ref[...], b_ref[...])


def _rw_masks():
    L = RW_CHUNK
    i = jnp.arange(L)
    r, c = i[:, None], i[None, :]
    masks = [r > c, r >= c]
    bsz = 1
    while bsz < L:
        masks.append((r // (2 * bsz) == c // (2 * bsz)) & ((r // bsz) % 2 == 1) & ((c // bsz) % 2 == 0))
        bsz *= 2
    return jnp.tile(jnp.stack(masks, 0).astype(F32), (1, 1, 2))


def _rwkv_ln(x, batch, p, g, b, tm=512):
    m = x.shape[0]
    seq = m // batch
    tm = min(tm, seq)
    nt = seq // tm
    tok = pl.BlockSpec((tm, D_MODEL), lambda i: (i, 0))
    prev = pl.BlockSpec((8, D_MODEL), lambda i: (jnp.maximum(i * (tm // 8) - 1, 0), 0))
    names = ['mu', 'w_r', 'w_k', 'w_v', 'w0', 'w1', 'w2', 'a0', 'a1', 'a2', 'g1', 'g2']
    ws = [p[n] for n in names]
    act = jax.ShapeDtypeStruct((m, D_MODEL), F32)
    r, k, v, lw, a, gate = pl.pallas_call(
        functools.partial(_rw_proj_kernel, tiles_per_seq=nt),
        grid=(m // tm,),
        in_specs=[tok, prev] + [_const_spec(w.shape) for w in ws],
        out_specs=[tok] * 6,
        out_shape=[act] * 6,
        compiler_params=_cparams("parallel"),
        name="rwkv_proj",
    )(x, x, *ws)

    lanes = RW_HB * RW_HEAD
    ts = min(RW_SCAN_TILE, seq)
    nts = seq // ts
    hblk = pl.BlockSpec((ts, lanes), lambda bi, hi, ti: (bi * nts + ti, hi))
    pblk = pl.BlockSpec((1, lanes), lambda bi, hi, ti: (0, hi))
    masks = _rw_masks()
    head_of_lane = jnp.arange(min(lanes, RW_SUM_LANES)) // RW_HEAD
    ones = (head_of_lane[:, None] == head_of_lane[None, :]).astype(BF16)
    bd = ones[:2 * RW_HEAD, :2 * RW_HEAD].astype(F32)
    y = pl.pallas_call(
        _rw_scan_kernel,
        grid=(batch, RW_HEADS // RW_HB, nts),
        in_specs=[hblk] * 5 + [pblk] * 5 + [_const_spec(masks.shape), _const_spec(ones.shape),
                                            _const_spec(bd.shape)],
        out_specs=hblk,
        out_shape=act,
        scratch_shapes=[pltpu.VMEM((RW_HB // 2, 2 * RW_HEAD, 2 * RW_HEAD), F32)],
        compiler_params=_cparams("parallel", "parallel", "arbitrary"),
        name="rwkv_scan",
    )(r, k, v, lw, a, p['k_k'], p['k_a'], p['r_k'], p['lnx_g'], p['lnx_b'], masks, ones, bd)

    return pl.pallas_call(
        _rw_out_kernel,
        grid=(m // tm,),
        in_specs=[tok, tok, tok, _const_spec(p['w_o'].shape), _const_spec(g.shape), _const_spec(b.shape)],
        out_specs=tok,
        out_shape=act,
        compiler_params=_cparams("parallel"),
        name="rwkv_out_ln",
    )(x, y, gate, p['w_o'], g, b)


def _lru_kernel(x_ref, win_ref, cw_ref, cb_ref, wa_ref, ba_ref, wx_ref, bx_ref, lam_ref, wout_ref,
                g_ref, b_ref, o_ref, xr_ref, a_ref, inp_ref, h_ref):
    W = D_MODEL
    tm = x_ref.shape[0]

    @pl.when(pl.program_id(1) == 0)
    def _():
        xr_ref[...] = jnp.zeros_like(xr_ref)
        h_ref[...] = jnp.zeros_like(h_ref)

    x = x_ref[...]
    proj = _dot(x.astype(BF16), win_ref[...])
    gate = jax.nn.gelu(proj[:, :W])
    xr_ref[0:8, :] = xr_ref[tm:tm + 8, :]
    xr_ref[8:tm + 8, :] = proj[:, W:]
    xc = cb_ref[...] + cw_ref[3:4, :] * xr_ref[8:tm + 8, :]
    for j in range(CONV_WIDTH - 1):
        xc = xc + cw_ref[j:j + 1, :] * xr_ref[pl.ds(5 + j, tm), :]
    xcb = xc.astype(BF16)
    gr = jnp.concatenate([_dot(xcb[:, kb * LRU_BLOCK:(kb + 1) * LRU_BLOCK], wa_ref[kb])
                          for kb in range(LRU_BLOCKS)], axis=-1) + ba_ref[...]
    gi = jnp.concatenate([_dot(xcb[:, kb * LRU_BLOCK:(kb + 1) * LRU_BLOCK], wx_ref[kb])
                          for kb in range(LRU_BLOCKS)], axis=-1) + bx_ref[...]
    log_a = -LRU_C * jax.nn.sigmoid(gr) * _softplus(-lam_ref[...])
    av = jnp.exp(log_a)
    a_ref[...] = av
    inp_ref[...] = jnp.sqrt(1.0 - av * av) * (jax.nn.sigmoid(gi) * xc)

    def block(i, h):
        rows = pl.ds(pl.multiple_of(i * 8, 8), 8)
        ab = a_ref[rows, :]
        bb = inp_ref[rows, :]
        out = []
        for j in range(8):
            h = ab[j:j + 1, :] * h + bb[j:j + 1, :]
            out.append(h)
        inp_ref[rows, :] = jnp.concatenate(out, axis=0)
        return h

    h_ref[...] = lax.fori_loop(0, tm // 8, block, h_ref[...])
    y = _dot((inp_ref[...] * gate).astype(BF16), wout_ref[...])
    o_ref[...] = _layer_norm(ALPHA * x + y, g_ref[...], b_ref[...])


def _lru_ln(x, batch, p, g, b, tm=256):
    m = x.shape[0]
    seq = m // batch
    tm = min(tm, seq)
    nt = seq // tm
    tok = pl.BlockSpec((tm, D_MODEL), lambda bi, ti: (bi * nt + ti, 0))
    names = ['w_in', 'conv_w', 'conv_b', 'w_a', 'b_a', 'w_x', 'b_x', 'lam', 'w_out']
    ws = [p[n] for n in names]
    return pl.pallas_call(
        _lru_kernel,
        grid=(batch, nt),
        in_specs=[tok] + [_const_spec(w.shape) for w in ws] + [_const_spec(g.shape), _const_spec(b.shape)],
        out_specs=tok,
        out_shape=jax.ShapeDtypeStruct((m, D_MODEL), F32),
        scratch_shapes=[pltpu.VMEM((tm + 8, D_MODEL), F32), pltpu.VMEM((tm, D_MODEL), F32),
                        pltpu.VMEM((tm, D_MODEL), F32), pltpu.VMEM((1, D_MODEL), F32)],
        compiler_params=_cparams("parallel", "arbitrary"),
        name="rglru_ln",
    )(x, *ws, g, b)


def kernel(x, ln_g, ln_b, ffn_w1, ffn_w3, ffn_w2, ret_w_in, ret_w_out, s5_w_in, s5_a_re, s5_a_im, s5_b_re, s5_b_im, s5_c_re, s5_c_im, s5_d, s5_log_step, s5_w_glu, s5_w_out, rw_mu, rw_w_r, rw_w_k, rw_w_v, rw_w0, rw_w1, rw_w2, rw_a0, rw_a1, rw_a2, rw_g1, rw_g2, rw_k_k, rw_k_a, rw_r_k, rw_lnx_g, rw_lnx_b, rw_w_o, lru_w_in, lru_conv_w, lru_conv_b, lru_w_a, lru_b_a, lru_w_x, lru_b_x, lru_lambda, lru_w_out):
    batch, seq, d = x.shape
    depth = ln_g.shape[0]
    h = x.reshape(batch * seq, d)
    bf = lambda t: t.astype(BF16)
    row = lambda t: t.reshape(1, -1)
    for i in range(depth):
        m, j = i % N_MIXERS, i // N_MIXERS
        h = _ffn_ln(h, bf(ffn_w1[i, 0]), bf(ffn_w3[i, 0]), bf(ffn_w2[i, 0]), row(ln_g[i, 0]), row(ln_b[i, 0]))
        g, b = row(ln_g[i, 1]), row(ln_b[i, 1])
        if m == 0:
            h = _retention_ln(h, batch, bf(ret_w_in[j]), bf(ret_w_out[j]), g, b)
        elif m == 1:
            ops = _s5_operators(s5_a_re[j], s5_a_im[j], s5_b_re[j], s5_b_im[j], s5_c_re[j], s5_c_im[j],
                                s5_log_step[j])
            h = _s5_ln(h, batch, bf(s5_w_in[j]), ops, row(s5_d[j]), bf(s5_w_glu[j]), bf(s5_w_out[j]), g, b)
        elif m == 2:
            p = dict(mu=rw_mu[j], w_r=bf(rw_w_r[j]), w_k=bf(rw_w_k[j]), w_v=bf(rw_w_v[j]),
                     w0=row(rw_w0[j]), w1=bf(rw_w1[j]), w2=bf(rw_w2[j]),
                     a0=row(rw_a0[j]), a1=bf(rw_a1[j]), a2=bf(rw_a2[j]), g1=bf(rw_g1[j]), g2=bf(rw_g2[j]),
                     k_k=row(rw_k_k[j]), k_a=row(rw_k_a[j]), r_k=row(rw_r_k[j]),
                     lnx_g=row(rw_lnx_g[j]), lnx_b=row(rw_lnx_b[j]), w_o=bf(rw_w_o[j]))
            h = _rwkv_ln(h, batch, p, g, b)
        else:
            p = dict(w_in=bf(lru_w_in[j]), conv_w=lru_conv_w[j], conv_b=row(lru_conv_b[j]),
                     w_a=bf(lru_w_a[j]), b_a=row(lru_b_a[j]), w_x=bf(lru_w_x[j]), b_x=row(lru_b_x[j]),
                     lam=row(lru_lambda[j]), w_out=bf(lru_w_out[j]))
            h = _lru_ln(h, batch, p, g, b)
        h = _ffn_ln(h, bf(ffn_w1[i, 1]), bf(ffn_w3[i, 1]), bf(ffn_w2[i, 1]), row(ln_g[i, 2]), row(ln_b[i, 2]))
    return h.reshape(batch, seq, d)
```

```python
import functools
import math

import jax
import jax.numpy as jnp
from jax import lax
from jax.experimental import pallas as pl
from jax.experimental.pallas import tpu as pltpu

F32 = jnp.float32
BF16 = jnp.bfloat16

D_MODEL = 1024
DEPTH = 4
N_MIXERS = 4
ALPHA = (2 * DEPTH) ** 0.25
LN_EPS = 1e-5
D_FF = 2816
FFN_HALF = 0.5

RET_HEADS = 4
RET_DK = 256
RET_DV = 512
ROPE_BASE = 10000.0
RET_GN_EPS = 1e-5

S5_GROUPS = 64
S5_GROUP = 16
S5_STATE = 64
S5_SUB = 16
S5_GB = 16
S5_NGB = S5_GROUPS // S5_GB

RW_HEADS = 16
RW_HEAD = 64
RW_GN_EPS = 64e-5
RW_CHUNK = 64
RW_SUM_LANES = 256
RW_SCAN_TILE = 256

LRU_BLOCKS = 4
LRU_BLOCK = 256
CONV_WIDTH = 4
LRU_C = 8.0
LRU_ROWS = 8

VMEM_LIMIT = 56 * 1024 * 1024


def _cparams(*sem):
    return pltpu.CompilerParams(dimension_semantics=sem, vmem_limit_bytes=VMEM_LIMIT)


def _const_spec(shape):
    nd = len(shape)
    return pl.BlockSpec(shape, lambda *_: (0,) * nd, pipeline_mode=pl.Buffered(1))


def _layer_norm(r, g, b):
    mu = jnp.mean(r, axis=-1, keepdims=True)
    d = r - mu
    var = jnp.mean(d * d, axis=-1, keepdims=True)
    return d * lax.rsqrt(var + LN_EPS) * g + b


def _dot(a, b):
    return jnp.dot(a, b, preferred_element_type=F32)


def _dot_bf(a, b):
    return jnp.dot(a.astype(BF16), b.astype(BF16), preferred_element_type=F32)


def _softplus(z):
    return jnp.maximum(z, 0.0) + jnp.log(1.0 + jnp.exp(-jnp.abs(z)))


def _ffn_kernel(x_ref, w1_ref, w3_ref, w2_ref, g_ref, b_ref, o_ref):
    x = x_ref[...]
    xb = x.astype(BF16)
    h1 = _dot(xb, w1_ref[...])
    h3 = _dot(xb, w3_ref[...])
    act = (h1 * jax.nn.sigmoid(h1) * h3).astype(BF16)
    y = _dot(act, w2_ref[...])
    o_ref[...] = _layer_norm(ALPHA * x + FFN_HALF * y, g_ref[...], b_ref[...])


def _ffn_ln(x, w1, w3, w2, g, b, tm=512):
    m = x.shape[0]
    tm = min(tm, m)
    tok = pl.BlockSpec((tm, D_MODEL), lambda i: (i, 0))
    return pl.pallas_call(
        _ffn_kernel,
        grid=(m // tm,),
        in_specs=[tok, _const_spec(w1.shape), _const_spec(w3.shape), _const_spec(w2.shape),
                  _const_spec(g.shape), _const_spec(b.shape)],
        out_specs=tok,
        out_shape=jax.ShapeDtypeStruct((m, D_MODEL), F32),
        compiler_params=_cparams("parallel"),
        name="ffn_ln",
    )(x, w1, w3, w2, g, b)


def _rotate(t1, t2, cos, sin):
    return jnp.concatenate([t1 * cos - t2 * sin, t1 * sin + t2 * cos], axis=-1)


def _ret_kernel(x_ref, cos_ref, sin_ref, win_ref, wout_ref, dmat_ref, qd_ref, kd_ref,
                g_ref, b_ref, o_ref, state_ref):
    @pl.when(pl.program_id(1) == 0)
    def _():
        state_ref[...] = jnp.zeros_like(state_ref)

    H, DK, DV = RET_HEADS, RET_DK, RET_DV
    half = DK // 2
    x = x_ref[...]
    tc = x.shape[0]
    proj = _dot(x.astype(BF16), win_ref[...])
    cos = cos_ref[...]
    sin = sin_ref[...]
    y = jnp.zeros_like(x)
    for h in range(H):
        q0 = h * DK
        k0 = H * DK + h * DK
        v0 = 2 * H * DK + h * DV
        g0 = 2 * H * DK + H * DV + h * DV
        qh = _rotate(proj[:, q0:q0 + half], proj[:, q0 + half:q0 + DK], cos, sin)
        kh = _rotate(proj[:, k0:k0 + half], proj[:, k0 + half:k0 + DK], cos, sin) * (DK ** -0.5)
        vh = proj[:, v0:v0 + DV].astype(BF16)
        gate = proj[:, g0:g0 + DV]
        s = lax.dot_general(qh.astype(BF16), kh.astype(BF16), (((1,), (1,)), ((), ())),
                            preferred_element_type=F32) * dmat_ref[h]
        state = state_ref[h]
        o = _dot(s.astype(BF16), vh) + _dot_bf(qh * qd_ref[h], state)
        kdec = (kh * kd_ref[h]).T.astype(BF16)
        chunk_decay = math.exp(tc * math.log1p(-(2.0 ** (-5.0 - h))))
        state_ref[h] = state * chunk_decay + _dot(kdec, vh)
        mu = jnp.mean(o, axis=-1, keepdims=True)
        d = o - mu
        var = jnp.mean(d * d, axis=-1, keepdims=True)
        o = d * lax.rsqrt(var + RET_GN_EPS)
        o = gate * jax.nn.sigmoid(gate) * o
        y = y + _dot(o.astype(BF16), wout_ref[h * DV:(h + 1) * DV, :])
    o_ref[...] = _layer_norm(ALPHA * x + y, g_ref[...], b_ref[...])


def _retention_ln(x, batch, w_in, w_out, g, b, tc=256):
    m = x.shape[0]
    seq = m // batch
    tc = min(tc, seq)
    nt = seq // tc
    H, DK = RET_HEADS, RET_DK
    half = DK // 2
    inv = ROPE_BASE ** (-jnp.arange(half, dtype=F32) / half)
    ang = jnp.arange(seq, dtype=F32)[:, None] * inv[None, :]
    cos, sin = jnp.cos(ang), jnp.sin(ang)
    log_gamma = jnp.log1p(-jnp.power(2.0, -5.0 - jnp.arange(H, dtype=F32)))
    pos = jnp.arange(tc, dtype=F32)
    rel = pos[:, None] - pos[None, :]
    dmat = jnp.where(rel >= 0, jnp.exp(jnp.maximum(rel, 0.0)[None] * log_gamma[:, None, None]), 0.0)
    qd = jnp.broadcast_to(jnp.exp((pos + 1.0)[None, :, None] * log_gamma[:, None, None]), (H, tc, DK))
    kd = jnp.broadcast_to(jnp.exp((tc - 1.0 - pos)[None, :, None] * log_gamma[:, None, None]), (H, tc, DK))

    tok = pl.BlockSpec((tc, D_MODEL), lambda bi, ti: (bi * nt + ti, 0))
    rope = pl.BlockSpec((tc, half), lambda bi, ti: (ti, 0))
    return pl.pallas_call(
        _ret_kernel,
        grid=(batch, nt),
        in_specs=[tok, rope, rope, _const_spec(w_in.shape), _const_spec(w_out.shape),
                  _const_spec(dmat.shape), _const_spec(qd.shape), _const_spec(kd.shape),
                  _const_spec(g.shape), _const_spec(b.shape)],
        out_specs=tok,
        out_shape=jax.ShapeDtypeStruct((m, D_MODEL), F32),
        scratch_shapes=[pltpu.VMEM((H, DK, RET_DV), F32)],
        compiler_params=_cparams("parallel", "arbitrary"),
        name="retention_ln",
    )(x, cos, sin, w_in, w_out, dmat, qd, kd, g, b)


def _s5_kernel(x_ref, perm_ref, permt_ref, win_ref, bre_ref, bim_ref, cre_ref, cim_ref, pwr_ref, pwi_ref, d_ref,
               wglu_ref, wout_ref, g_ref, b_ref, o_ref, hre_ref, him_ref, ere_ref, eim_ref,
               sre_ref, sim_ref, y_ref):
    @pl.when(pl.program_id(1) == 0)
    def _():
        sre_ref[...] = jnp.zeros_like(sre_ref)
        sim_ref[...] = jnp.zeros_like(sim_ref)

    L = S5_SUB
    tm = x_ref.shape[0]
    nr = tm // L
    cols = S5_GB * S5_GROUP
    x = x_ref[...]
    xp = _dot(perm_ref[...], x.astype(BF16)).astype(BF16)
    u = _dot(xp, win_ref[...])
    ub = u.astype(BF16)
    for gb in range(S5_NGB):
        ug = ub[:, gb * cols:(gb + 1) * cols]
        hre_ref[...] = _dot(ug, bre_ref[gb])
        him_ref[...] = _dot(ug, bim_ref[gb])
        sub = lambda j: slice(j * nr, (j + 1) * nr)
        ar = pwr_ref[gb, 1:2, :]
        ai = pwi_ref[gb, 1:2, :]
        hr = hre_ref[sub(0), :]
        hi = him_ref[sub(0), :]
        for j in range(1, L):
            hr, hi = (ar * hr - ai * hi + hre_ref[sub(j), :], ar * hi + ai * hr + him_ref[sub(j), :])
            hre_ref[sub(j), :] = hr
            him_ref[sub(j), :] = hi
        ere_ref[...] = hr
        eim_ref[...] = hi
        alr = pwr_ref[gb, L:L + 1, :]
        ali = pwi_ref[gb, L:L + 1, :]

        def row(c, carry):
            sr, si = carry
            er = ere_ref[pl.ds(c, 1), :]
            ei = eim_ref[pl.ds(c, 1), :]
            ere_ref[pl.ds(c, 1), :] = sr
            eim_ref[pl.ds(c, 1), :] = si
            return alr * sr - ali * si + er, alr * si + ali * sr + ei

        sr, si = lax.fori_loop(0, nr, row, (sre_ref[gb], sim_ref[gb]))
        sre_ref[gb] = sr
        sim_ref[gb] = si
        pr_in = ere_ref[...]
        pi_in = eim_ref[...]
        for j in range(L):
            pr = pwr_ref[gb, j + 1:j + 2, :]
            pi = pwi_ref[gb, j + 1:j + 2, :]
            hre_ref[sub(j), :] = hre_ref[sub(j), :] + (pr * pr_in - pi * pi_in)
            him_ref[sub(j), :] = him_ref[sub(j), :] + (pr * pi_in + pi * pr_in)
        y_ref[:, gb * cols:(gb + 1) * cols] = (_dot_bf(hre_ref[...], cre_ref[gb])
                                               + _dot_bf(him_ref[...], cim_ref[gb]))
    yv = y_ref[...] + d_ref[...] * u
    act = jax.nn.gelu(yv)
    z = act * jax.nn.sigmoid(_dot(act.astype(BF16), wglu_ref[...]))
    out = _dot(z.astype(BF16), wout_ref[...])
    hi16 = out.astype(BF16)
    lo16 = (out - hi16.astype(F32)).astype(BF16)
    out = _dot(permt_ref[...], hi16) + _dot(permt_ref[...], lo16)
    o_ref[...] = _layer_norm(ALPHA * x + out, g_ref[...], b_ref[...])


def _s5_operators(a_re, a_im, b_re, b_im, c_re, c_im, log_step):
    L = S5_SUB
    G, N, P = b_re.shape
    dt = jnp.exp(log_step)[:, None]
    mag = jnp.exp(dt * a_re)
    abar_re = mag * jnp.cos(dt * a_im)
    abar_im = mag * jnp.sin(dt * a_im)
    den = a_re * a_re + a_im * a_im
    f_re = ((abar_re - 1.0) * a_re + abar_im * a_im) / den
    f_im = (abar_im * a_re - (abar_re - 1.0) * a_im) / den
    bb_re = f_re[..., None] * b_re - f_im[..., None] * b_im
    bb_im = f_re[..., None] * b_im + f_im[..., None] * b_re
    pr, pi = [jnp.ones_like(abar_re)], [jnp.zeros_like(abar_re)]
    for _ in range(L):
        pr.append(pr[-1] * abar_re - pi[-1] * abar_im)
        pi.append(pr[-2] * abar_im + pi[-1] * abar_re)
    gbn, ngb = S5_GB, S5_NGB
    eye = jnp.eye(gbn, dtype=F32)
    blockdiag_in = lambda t: jnp.einsum('bgnp,gh->bgphn', t.reshape(ngb, gbn, N, P), eye).reshape(
        ngb, gbn * P, gbn * N)
    blockdiag_out = lambda t: jnp.einsum('bgpn,gh->bgnhp', t.reshape(ngb, gbn, P, N), eye).reshape(
        ngb, gbn * N, gbn * P)
    planes = lambda t: jnp.stack(t, 0).reshape(L + 1, ngb, gbn * N).transpose(1, 0, 2)
    return (blockdiag_in(bb_re).astype(BF16), blockdiag_in(bb_im).astype(BF16),
            blockdiag_out(c_re).astype(BF16), blockdiag_out(-c_im).astype(BF16), planes(pr), planes(pi))


def _s5_ln(x, batch, w_in, ops, d_skip, w_glu, w_out, g, b, tm=512):
    m = x.shape[0]
    seq = m // batch
    tm = min(tm, seq)
    nt = seq // tm
    nr = tm // S5_SUB
    rows = jnp.arange(tm)
    perm = (rows[None, :] == (rows[:, None] % nr) * S5_SUB + rows[:, None] // nr).astype(BF16)
    ws = [perm, perm.T, w_in, *ops, d_skip, w_glu, w_out, g, b]
    tok = pl.BlockSpec((tm, D_MODEL), lambda bi, ti: (bi * nt + ti, 0))
    wide = S5_GB * S5_STATE
    return pl.pallas_call(
        _s5_kernel,
        grid=(batch, nt),
        in_specs=[tok] + [_const_spec(w.shape) for w in ws],
        out_specs=tok,
        out_shape=jax.ShapeDtypeStruct((m, D_MODEL), F32),
        scratch_shapes=[pltpu.VMEM((tm, wide), F32), pltpu.VMEM((tm, wide), F32),
                        pltpu.VMEM((nr, wide), F32), pltpu.VMEM((nr, wide), F32),
                        pltpu.VMEM((S5_NGB, 1, wide), F32), pltpu.VMEM((S5_NGB, 1, wide), F32),
                        pltpu.VMEM((tm, D_MODEL), F32)],
        compiler_params=_cparams("parallel", "arbitrary"),
        name="s5_ln",
    )(x, *ws)


def _rw_proj_kernel(x_ref, xp_ref, mu_ref, wr_ref, wk_ref, wv_ref, w0_ref, w1_ref, w2_ref,
                    a0_ref, a1_ref, a2_ref, g1_ref, g2_ref,
                    r_ref, k_ref, v_ref, lw_ref, a_ref, g_ref, *, tiles_per_seq):
    x = x_ref[...]
    first = (pl.program_id(0) % tiles_per_seq) == 0
    prev = jnp.where(first, 0.0, xp_ref[7:8, :])
    rows = lax.broadcasted_iota(jnp.int32, x.shape, 0)
    shifted = jnp.where(rows == 0, prev, pltpu.roll(x, 1, 0))
    xx = shifted - x
    mix = lambda i: (x + xx * mu_ref[i:i + 1, :]).astype(BF16)
    r_ref[...] = _dot(mix(0), wr_ref[...])
    z = w0_ref[...] + _dot_bf(jnp.tanh(_dot(mix(1), w1_ref[...])), w2_ref[...])
    w = -_softplus(-z) - 0.5
    lw_ref[...] = -jnp.exp(w)
    k_ref[...] = _dot(mix(2), wk_ref[...])
    v_ref[...] = _dot(mix(3), wv_ref[...])
    a_ref[...] = jax.nn.sigmoid(a0_ref[...] + _dot_bf(_dot(mix(4), a1_ref[...]), a2_ref[...]))
    g_ref[...] = _dot_bf(jax.nn.sigmoid(_dot(mix(5), g1_ref[...])), g2_ref[...])


def _rw_scan_kernel(r_ref, k_ref, v_ref, lw_ref, a_ref, kk_ref, ka_ref, rk_ref, lg_ref, lb_ref,
                    mask_ref, ones_ref, bd_ref, x_ref, gate_ref, wo_ref, g_ref, b_ref, o_ref, state_ref):
    @pl.when(pl.program_id(1) == 0)
    def _():
        state_ref[...] = jnp.zeros_like(state_ref)

    L, N = RW_CHUNK, RW_HEAD
    PW = 2 * N
    tm, width = r_ref.shape
    nc, npair = tm // L, width // PW
    nt = (((1,), (1,)), ((), ()))
    tn = (((0,), (0,)), ((), ()))
    strict = mask_ref[0]
    incl = mask_ref[1]
    rr = lax.broadcasted_iota(jnp.int32, (L, PW), 0)
    cc = lax.broadcasted_iota(jnp.int32, (L, PW), 1)
    eye = jnp.where(rr == (cc & (N - 1)), 1.0, 0.0).astype(F32)
    first = cc < N
    bd = bd_ref[...]

    def bdiag(t):
        zero = jnp.zeros_like(t)
        return jnp.concatenate([jnp.where(first, t, zero), jnp.where(first, zero, t)], axis=0)

    lw = lw_ref[...]
    pos = lax.broadcasted_iota(jnp.int32, lw.shape, 0) & (L - 1)
    cum = lw
    step = 1
    while step < L:
        cum = cum + jnp.where(pos >= step, pltpu.roll(cum, step, 0), 0.0)
        step *= 2
    g_in = jnp.exp(cum)
    g_ex = jnp.exp(cum - lw)
    g_inv = jnp.exp(-cum)
    r = r_ref[...]
    k = k_ref[...]
    a = a_ref[...]

    def head_sum(t):
        hw = ones_ref.shape[0]
        hi = t.astype(BF16)
        lo = (t - hi.astype(F32)).astype(BF16)
        return jnp.concatenate(
            [_dot(hi[:, j:j + hw], ones_ref[...]) + _dot(lo[:, j:j + hw], ones_ref[...])
             for j in range(0, t.shape[1], hw)], axis=-1)

    kkr = k * kk_ref[...]
    kk_all = kkr / jnp.maximum(jnp.sqrt(head_sum(kkr * kkr)), 1e-12)
    k = k * (1.0 + (a - 1.0) * ka_ref[...])
    bonus = head_sum(r * k * rk_ref[...]) * v_ref[...]
    at_all = (-kk_all * g_ex).astype(BF16)
    bt_all = kk_all * a * g_inv
    rt_all = (r * g_in).astype(BF16)
    kt_all = k * g_inv
    vb_all = v_ref[...].astype(BF16)

    chains = [(p, c) for p in range(npair) for c in range(nc)]
    sl = lambda t, i: t[i[1] * L:(i[1] + 1) * L, i[0] * PW:(i[0] + 1) * PW]
    atc = {i: sl(at_all, i) for i in chains}
    rtc = {i: sl(rt_all, i) for i in chains}
    ktc = {i: sl(kt_all, i) for i in chains}
    btc = {i: sl(bt_all, i) for i in chains}
    vbd = {i: bdiag(sl(vb_all, i)) for i in chains}
    gl = {i: g_in[(i[1] + 1) * L - 1:(i[1] + 1) * L, i[0] * PW:(i[0] + 1) * PW] for i in chains}
    ar = {i: jnp.concatenate([atc[i], rtc[i]], axis=0) for i in chains}
    gab = {i: lax.dot_general(ar[i], bdiag(btc[i].astype(BF16)), nt, preferred_element_type=F32)
           for i in chains}
    gak = {i: lax.dot_general(ar[i], bdiag(ktc[i].astype(BF16)), nt, preferred_element_type=F32)
           for i in chains}
    a_ab = {i: gab[i][:L] * strict for i in chains}
    b_rb = {i: (gab[i][L:] * incl).astype(BF16) for i in chains}
    a_ak = {i: (gak[i][:L] * strict).astype(BF16) for i in chains}
    b_rk = {i: (gak[i][L:] * incl).astype(BF16) for i in chains}
    tinv = {i: eye + a_ab[i] * mask_ref[2] for i in chains}
    lvl, bsz = 1, 2
    while bsz < L:
        mask = mask_ref[2 + lvl]
        half = {i: _dot((a_ab[i] * mask).astype(BF16), bdiag(tinv[i].astype(BF16))) for i in chains}
        tinv = {i: tinv[i] + _dot(tinv[i].astype(BF16), bdiag(half[i].astype(BF16))) for i in chains}
        bsz *= 2
        lvl += 1
    tb = {i: tinv[i].astype(BF16) for i in chains}
    ap = {i: _dot(tb[i], bdiag(atc[i])).astype(BF16) for i in chains}
    akv = {i: _dot(a_ak[i], vbd[i]).astype(BF16) for i in chains}
    wloc = {i: _dot(tb[i], bdiag(akv[i])) for i in chains}
    bv = {i: _dot(b_rk[i], vbd[i]) for i in chains}
    bg = {i: (btc[i] * gl[i]).astype(BF16) for i in chains}
    kv = {i: lax.dot_general(sl(vb_all, i), (ktc[i] * gl[i]).astype(BF16), tn,
                             preferred_element_type=F32) * bd for i in chains}
    apr = {i: jnp.concatenate([ap[i], rtc[i]], axis=0) for i in chains}

    states = [state_ref[p] for p in range(npair)]
    outs = [[] for _ in range(npair)]
    for c in range(nc):
        both = [lax.dot_general(apr[p, c], states[p].astype(BF16), nt, preferred_element_type=F32)
                for p in range(npair)]
        ub = [(both[p][:L] + wloc[p, c]).astype(BF16) for p in range(npair)]
        for p in range(npair):
            states[p] = (states[p] * gl[p, c] + kv[p, c]
                         + lax.dot_general(ub[p], bg[p, c], tn, preferred_element_type=F32) * bd)
        for p in range(npair):
            outs[p].append(both[p][L:] + _dot(b_rb[p, c], bdiag(ub[p])) + bv[p, c])

    for p in range(npair):
        state_ref[p] = states[p]
    o = jnp.concatenate([jnp.concatenate(outs[p], axis=0) for p in range(npair)], axis=-1)
    mu = head_sum(o) * (1.0 / N)
    d = o - mu
    var = head_sum(d * d) * (1.0 / N)
    y = d * lax.rsqrt(var + RW_GN_EPS) * lg_ref[...] + lb_ref[...] + bonus
    out = _dot((y * gate_ref[...]).astype(BF16), wo_ref[...])
    o_ref[...] = _layer_norm(ALPHA * x_ref[...] + out, g_ref[...], b_ref[...])


def _rw_masks():
    L = RW_CHUNK
    i = jnp.arange(L)
    r, c = i[:, None], i[None, :]
    masks = [r > c, r >= c]
    bsz = 1
    while bsz < L:
        masks.append((r // (2 * bsz) == c // (2 * bsz)) & ((r // bsz) % 2 == 1) & ((c // bsz) % 2 == 0))
        bsz *= 2
    return jnp.tile(jnp.stack(masks, 0).astype(F32), (1, 1, 2))


def _rwkv_ln(x, batch, p, g, b, tm=512):
    m = x.shape[0]
    seq = m // batch
    tm = min(tm, seq)
    nt = seq // tm
    tok = pl.BlockSpec((tm, D_MODEL), lambda i: (i, 0))
    prev = pl.BlockSpec((8, D_MODEL), lambda i: (jnp.maximum(i * (tm // 8) - 1, 0), 0))
    names = ['mu', 'w_r', 'w_k', 'w_v', 'w0', 'w1', 'w2', 'a0', 'a1', 'a2', 'g1', 'g2']
    ws = [p[n] for n in names]
    act = jax.ShapeDtypeStruct((m, D_MODEL), F32)
    r, k, v, lw, a, gate = pl.pallas_call(
        functools.partial(_rw_proj_kernel, tiles_per_seq=nt),
        grid=(m // tm,),
        in_specs=[tok, prev] + [_const_spec(w.shape) for w in ws],
        out_specs=[tok] * 6,
        out_shape=[act] * 6,
        compiler_params=_cparams("parallel"),
        name="rwkv_proj",
    )(x, x, *ws)

    ts = min(RW_SCAN_TILE, seq)
    nts = seq // ts
    stok = pl.BlockSpec((ts, D_MODEL), lambda bi, ti: (bi * nts + ti, 0))
    masks = _rw_masks()
    head_of_lane = jnp.arange(RW_SUM_LANES) // RW_HEAD
    ones = (head_of_lane[:, None] == head_of_lane[None, :]).astype(BF16)
    bd = ones[:2 * RW_HEAD, :2 * RW_HEAD].astype(F32)
    consts = [p['k_k'], p['k_a'], p['r_k'], p['lnx_g'], p['lnx_b'], masks, ones, bd]
    tail = [p['w_o'], g, b]
    return pl.pallas_call(
        _rw_scan_kernel,
        grid=(batch, nts),
        in_specs=([stok] * 5 + [_const_spec(c.shape) for c in consts] + [stok, stok]
                  + [_const_spec(c.shape) for c in tail]),
        out_specs=stok,
        out_shape=act,
        scratch_shapes=[pltpu.VMEM((RW_HEADS // 2, 2 * RW_HEAD, 2 * RW_HEAD), F32)],
        compiler_params=_cparams("parallel", "arbitrary"),
        name="rwkv_scan_ln",
    )(r, k, v, lw, a, *consts, x, gate, *tail)


def _lru_kernel(x_ref, win_ref, cw_ref, cb_ref, wa_ref, ba_ref, wx_ref, bx_ref, lam_ref, wout_ref,
                g_ref, b_ref, o_ref, xr_ref, h_ref):
    W = D_MODEL
    tm = x_ref.shape[0]

    @pl.when(pl.program_id(1) == 0)
    def _():
        xr_ref[...] = jnp.zeros_like(xr_ref)
        h_ref[...] = jnp.zeros_like(h_ref)

    x = x_ref[...]
    proj = _dot(x.astype(BF16), win_ref[...])
    gate = jax.nn.gelu(proj[:, :W])
    xr_ref[0:8, :] = xr_ref[tm:tm + 8, :]
    xr_ref[8:tm + 8, :] = proj[:, W:]
    xc = cb_ref[...] + cw_ref[3:4, :] * xr_ref[8:tm + 8, :]
    for j in range(CONV_WIDTH - 1):
        xc = xc + cw_ref[j:j + 1, :] * xr_ref[pl.ds(5 + j, tm), :]
    xcb = xc.astype(BF16)
    gr = jnp.concatenate([_dot(xcb[:, kb * LRU_BLOCK:(kb + 1) * LRU_BLOCK], wa_ref[kb])
                          for kb in range(LRU_BLOCKS)], axis=-1) + ba_ref[...]
    gi = jnp.concatenate([_dot(xcb[:, kb * LRU_BLOCK:(kb + 1) * LRU_BLOCK], wx_ref[kb])
                          for kb in range(LRU_BLOCKS)], axis=-1) + bx_ref[...]
    log_a = -LRU_C * jax.nn.sigmoid(gr) * _softplus(-lam_ref[...])
    av = jnp.exp(log_a)
    bv = jnp.sqrt(1.0 - av * av) * (jax.nn.sigmoid(gi) * xc)
    pos = lax.broadcasted_iota(jnp.int32, av.shape, 0) & (LRU_ROWS - 1)
    step = 1
    while step < LRU_ROWS:
        keep = pos >= step
        bv = av * jnp.where(keep, pltpu.roll(bv, step, 0), 0.0) + bv
        av = av * jnp.where(keep, pltpu.roll(av, step, 0), 1.0)
        step *= 2
    nb = tm // LRU_ROWS
    a_end = av.reshape(nb, LRU_ROWS, W)[:, LRU_ROWS - 1, :]
    b_end = bv.reshape(nb, LRU_ROWS, W)[:, LRU_ROWS - 1, :]
    h = h_ref[...]
    carries = []
    for i in range(nb):
        carries.append(h)
        h = a_end[i:i + 1] * h + b_end[i:i + 1]
    h_ref[...] = h
    carry = jnp.concatenate(carries, axis=0)
    hs = av * jnp.broadcast_to(carry[:, None, :], (nb, LRU_ROWS, W)).reshape(tm, W) + bv
    y = _dot((hs * gate).astype(BF16), wout_ref[...])
    o_ref[...] = _layer_norm(ALPHA * x + y, g_ref[...], b_ref[...])


def _lru_ln(x, batch, p, g, b, tm=256):
    m = x.shape[0]
    seq = m // batch
    tm = min(tm, seq)
    nt = seq // tm
    tok = pl.BlockSpec((tm, D_MODEL), lambda bi, ti: (bi * nt + ti, 0))
    names = ['w_in', 'conv_w', 'conv_b', 'w_a', 'b_a', 'w_x', 'b_x', 'lam', 'w_out']
    ws = [p[n] for n in names]
    return pl.pallas_call(
        _lru_kernel,
        grid=(batch, nt),
        in_specs=[tok] + [_const_spec(w.shape) for w in ws] + [_const_spec(g.shape), _const_spec(b.shape)],
        out_specs=tok,
        out_shape=jax.ShapeDtypeStruct((m, D_MODEL), F32),
        scratch_shapes=[pltpu.VMEM((tm + 8, D_MODEL), F32), pltpu.VMEM((1, D_MODEL), F32)],
        compiler_params=_cparams("parallel", "arbitrary"),
        name="rglru_ln",
    )(x, *ws, g, b)


def kernel(x, ln_g, ln_b, ffn_w1, ffn_w3, ffn_w2, ret_w_in, ret_w_out, s5_w_in, s5_a_re, s5_a_im, s5_b_re, s5_b_im, s5_c_re, s5_c_im, s5_d, s5_log_step, s5_w_glu, s5_w_out, rw_mu, rw_w_r, rw_w_k, rw_w_v, rw_w0, rw_w1, rw_w2, rw_a0, rw_a1, rw_a2, rw_g1, rw_g2, rw_k_k, rw_k_a, rw_r_k, rw_lnx_g, rw_lnx_b, rw_w_o, lru_w_in, lru_conv_w, lru_conv_b, lru_w_a, lru_b_a, lru_w_x, lru_b_x, lru_lambda, lru_w_out):
    batch, seq, d = x.shape
    depth = ln_g.shape[0]
    h = x.reshape(batch * seq, d)
    bf = lambda t: t.astype(BF16)
    row = lambda t: t.reshape(1, -1)
    for i in range(depth):
        m, j = i % N_MIXERS, i // N_MIXERS
        h = _ffn_ln(h, bf(ffn_w1[i, 0]), bf(ffn_w3[i, 0]), bf(ffn_w2[i, 0]), row(ln_g[i, 0]), row(ln_b[i, 0]))
        g, b = row(ln_g[i, 1]), row(ln_b[i, 1])
        if m == 0:
            h = _retention_ln(h, batch, bf(ret_w_in[j]), bf(ret_w_out[j]), g, b)
        elif m == 1:
            ops = _s5_operators(s5_a_re[j], s5_a_im[j], s5_b_re[j], s5_b_im[j], s5_c_re[j], s5_c_im[j],
                                s5_log_step[j])
            h = _s5_ln(h, batch, bf(s5_w_in[j]), ops, row(s5_d[j]), bf(s5_w_glu[j]), bf(s5_w_out[j]), g, b)
        elif m == 2:
            p = dict(mu=rw_mu[j], w_r=bf(rw_w_r[j]), w_k=bf(rw_w_k[j]), w_v=bf(rw_w_v[j]),
                     w0=row(rw_w0[j]), w1=bf(rw_w1[j]), w2=bf(rw_w2[j]),
                     a0=row(rw_a0[j]), a1=bf(rw_a1[j]), a2=bf(rw_a2[j]), g1=bf(rw_g1[j]), g2=bf(rw_g2[j]),
                     k_k=row(rw_k_k[j]), k_a=row(rw_k_a[j]), r_k=row(rw_r_k[j]),
                     lnx_g=row(rw_lnx_g[j]), lnx_b=row(rw_lnx_b[j]), w_o=bf(rw_w_o[j]))
            h = _rwkv_ln(h, batch, p, g, b)
        else:
            p = dict(w_in=bf(lru_w_in[j]), conv_w=lru_conv_w[j], conv_b=row(lru_conv_b[j]),
                     w_a=bf(lru_w_a[j]), b_a=row(lru_b_a[j]), w_x=bf(lru_w_x[j]), b_x=row(lru_b_x[j]),
                     lam=row(lru_lambda[j]), w_out=bf(lru_w_out[j]))
            h = _lru_ln(h, batch, p, g, b)
        h = _ffn_ln(h, bf(ffn_w1[i, 1]), bf(ffn_w3[i, 1]), bf(ffn_w2[i, 1]), row(ln_g[i, 2]), row(ln_b[i, 2]))
    return h.reshape(batch, seq, d)
```

```python
import functools
import math

import jax
import jax.numpy as jnp
from jax import lax
from jax.experimental import pallas as pl
from jax.experimental.pallas import tpu as pltpu

F32 = jnp.float32
BF16 = jnp.bfloat16

D_MODEL = 1024
DEPTH = 4
N_MIXERS = 4
ALPHA = (2 * DEPTH) ** 0.25
LN_EPS = 1e-5
D_FF = 2816
FFN_HALF = 0.5

RET_HEADS = 4
RET_DK = 256
RET_DV = 512
ROPE_BASE = 10000.0
RET_GN_EPS = 1e-5

S5_GROUPS = 64
S5_GROUP = 16
S5_STATE = 64
S5_SUB = 16
S5_GB = 16
S5_NGB = S5_GROUPS // S5_GB

RW_HEADS = 16
RW_HEAD = 64
RW_GN_EPS = 64e-5
RW_CHUNK = 64
RW_SUM_LANES = 256
RW_SCAN_TILE = 256

LRU_BLOCKS = 4
LRU_BLOCK = 256
CONV_WIDTH = 4
LRU_C = 8.0
LRU_ROWS = 8

VMEM_LIMIT = 56 * 1024 * 1024


def _cparams(*sem):
    return pltpu.CompilerParams(dimension_semantics=sem, vmem_limit_bytes=VMEM_LIMIT)


def _const_spec(shape):
    nd = len(shape)
    return pl.BlockSpec(shape, lambda *_: (0,) * nd, pipeline_mode=pl.Buffered(1))


def _layer_norm(r, g, b):
    mu = jnp.mean(r, axis=-1, keepdims=True)
    d = r - mu
    var = jnp.mean(d * d, axis=-1, keepdims=True)
    return d * lax.rsqrt(var + LN_EPS) * g + b


def _dot(a, b):
    return jnp.dot(a, b, preferred_element_type=F32)


def _dot_bf(a, b):
    return jnp.dot(a.astype(BF16), b.astype(BF16), preferred_element_type=F32)


def _softplus(z):
    return jnp.maximum(z, 0.0) + jnp.log(1.0 + jnp.exp(-jnp.abs(z)))


def _ffn_kernel(x_ref, w1_ref, w3_ref, w2_ref, g_ref, b_ref, o_ref):
    x = x_ref[...]
    xb = x.astype(BF16)
    h1 = _dot(xb, w1_ref[...])
    h3 = _dot(xb, w3_ref[...])
    act = (h1 * jax.nn.sigmoid(h1) * h3).astype(BF16)
    y = _dot(act, w2_ref[...])
    o_ref[...] = _layer_norm(ALPHA * x + FFN_HALF * y, g_ref[...], b_ref[...])


def _ffn_ln(x, w1, w3, w2, layer, slot, g, b, tm=512):
    m = x.shape[0]
    tm = min(tm, m)
    tok = pl.BlockSpec((tm, D_MODEL), lambda i: (i, 0))
    pick = lambda w: pl.BlockSpec((None, None) + w.shape[2:], lambda i: (layer, slot, 0, 0),
                                  pipeline_mode=pl.Buffered(1))
    return pl.pallas_call(
        _ffn_kernel,
        grid=(m // tm,),
        in_specs=[tok, pick(w1), pick(w3), pick(w2), _const_spec(g.shape), _const_spec(b.shape)],
        out_specs=tok,
        out_shape=jax.ShapeDtypeStruct((m, D_MODEL), F32),
        compiler_params=_cparams("parallel"),
        name="ffn_ln",
    )(x, w1, w3, w2, g, b)


def _rotate(t1, t2, cos, sin):
    return jnp.concatenate([t1 * cos - t2 * sin, t1 * sin + t2 * cos], axis=-1)


def _ret_kernel(x_ref, cos_ref, sin_ref, win_ref, wout_ref, dmat_ref, qd_ref, kd_ref,
                g_ref, b_ref, o_ref, state_ref):
    @pl.when(pl.program_id(1) == 0)
    def _():
        state_ref[...] = jnp.zeros_like(state_ref)

    H, DK, DV = RET_HEADS, RET_DK, RET_DV
    half = DK // 2
    x = x_ref[...]
    tc = x.shape[0]
    proj = _dot(x.astype(BF16), win_ref[...])
    cos = cos_ref[...]
    sin = sin_ref[...]
    y = jnp.zeros_like(x)
    for h in range(H):
        q0 = h * DK
        k0 = H * DK + h * DK
        v0 = 2 * H * DK + h * DV
        g0 = 2 * H * DK + H * DV + h * DV
        qh = _rotate(proj[:, q0:q0 + half], proj[:, q0 + half:q0 + DK], cos, sin)
        kh = _rotate(proj[:, k0:k0 + half], proj[:, k0 + half:k0 + DK], cos, sin) * (DK ** -0.5)
        vh = proj[:, v0:v0 + DV].astype(BF16)
        gate = proj[:, g0:g0 + DV]
        s = lax.dot_general(qh.astype(BF16), kh.astype(BF16), (((1,), (1,)), ((), ())),
                            preferred_element_type=F32) * dmat_ref[h]
        state = state_ref[h]
        o = _dot(s.astype(BF16), vh) + _dot_bf(qh * qd_ref[h], state)
        kdec = (kh * kd_ref[h]).T.astype(BF16)
        chunk_decay = math.exp(tc * math.log1p(-(2.0 ** (-5.0 - h))))
        state_ref[h] = state * chunk_decay + _dot(kdec, vh)
        mu = jnp.mean(o, axis=-1, keepdims=True)
        d = o - mu
        var = jnp.mean(d * d, axis=-1, keepdims=True)
        o = d * lax.rsqrt(var + RET_GN_EPS)
        o = gate * jax.nn.sigmoid(gate) * o
        y = y + _dot(o.astype(BF16), wout_ref[h * DV:(h + 1) * DV, :])
    o_ref[...] = _layer_norm(ALPHA * x + y, g_ref[...], b_ref[...])


def _retention_ln(x, batch, w_in, w_out, g, b, tc=512):
    m = x.shape[0]
    seq = m // batch
    tc = min(tc, seq)
    nt = seq // tc
    H, DK = RET_HEADS, RET_DK
    half = DK // 2
    inv = ROPE_BASE ** (-jnp.arange(half, dtype=F32) / half)
    ang = jnp.arange(seq, dtype=F32)[:, None] * inv[None, :]
    cos, sin = jnp.cos(ang), jnp.sin(ang)
    log_gamma = jnp.log1p(-jnp.power(2.0, -5.0 - jnp.arange(H, dtype=F32)))
    pos = jnp.arange(tc, dtype=F32)
    rel = pos[:, None] - pos[None, :]
    dmat = jnp.where(rel >= 0, jnp.exp(jnp.maximum(rel, 0.0)[None] * log_gamma[:, None, None]), 0.0)
    qd = jnp.broadcast_to(jnp.exp((pos + 1.0)[None, :, None] * log_gamma[:, None, None]), (H, tc, DK))
    kd = jnp.broadcast_to(jnp.exp((tc - 1.0 - pos)[None, :, None] * log_gamma[:, None, None]), (H, tc, DK))

    tok = pl.BlockSpec((tc, D_MODEL), lambda bi, ti: (bi * nt + ti, 0))
    rope = pl.BlockSpec((tc, half), lambda bi, ti: (ti, 0))
    return pl.pallas_call(
        _ret_kernel,
        grid=(batch, nt),
        in_specs=[tok, rope, rope, _const_spec(w_in.shape), _const_spec(w_out.shape),
                  _const_spec(dmat.shape), _const_spec(qd.shape), _const_spec(kd.shape),
                  _const_spec(g.shape), _const_spec(b.shape)],
        out_specs=tok,
        out_shape=jax.ShapeDtypeStruct((m, D_MODEL), F32),
        scratch_shapes=[pltpu.VMEM((H, DK, RET_DV), F32)],
        compiler_params=_cparams("parallel", "arbitrary"),
        name="retention_ln",
    )(x, cos, sin, w_in, w_out, dmat, qd, kd, g, b)


def _s5_kernel(x_ref, perm_ref, permt_ref, win_ref, bre_ref, bim_ref, cre_ref, cim_ref, pwr_ref, pwi_ref, d_ref,
               wglu_ref, wout_ref, g_ref, b_ref, o_ref, hre_ref, him_ref, ere_ref, eim_ref,
               sre_ref, sim_ref, y_ref):
    @pl.when(pl.program_id(1) == 0)
    def _():
        sre_ref[...] = jnp.zeros_like(sre_ref)
        sim_ref[...] = jnp.zeros_like(sim_ref)

    L = S5_SUB
    tm = x_ref.shape[0]
    nr = tm // L
    cols = S5_GB * S5_GROUP
    x = x_ref[...]
    xp = _dot(perm_ref[...], x.astype(BF16)).astype(BF16)
    u = _dot(xp, win_ref[...])
    ub = u.astype(BF16)
    for gb in range(S5_NGB):
        ug = ub[:, gb * cols:(gb + 1) * cols]
        hre_ref[...] = _dot(ug, bre_ref[gb])
        him_ref[...] = _dot(ug, bim_ref[gb])
        sub = lambda j: slice(j * nr, (j + 1) * nr)
        ar = pwr_ref[gb, 1:2, :]
        ai = pwi_ref[gb, 1:2, :]
        hr = hre_ref[sub(0), :]
        hi = him_ref[sub(0), :]
        for j in range(1, L):
            hr, hi = (ar * hr - ai * hi + hre_ref[sub(j), :], ar * hi + ai * hr + him_ref[sub(j), :])
            hre_ref[sub(j), :] = hr
            him_ref[sub(j), :] = hi
        ere_ref[...] = hr
        eim_ref[...] = hi
        alr = pwr_ref[gb, L:L + 1, :]
        ali = pwi_ref[gb, L:L + 1, :]

        def row(c, carry):
            sr, si = carry
            er = ere_ref[pl.ds(c, 1), :]
            ei = eim_ref[pl.ds(c, 1), :]
            ere_ref[pl.ds(c, 1), :] = sr
            eim_ref[pl.ds(c, 1), :] = si
            return alr * sr - ali * si + er, alr * si + ali * sr + ei

        sr, si = lax.fori_loop(0, nr, row, (sre_ref[gb], sim_ref[gb]))
        sre_ref[gb] = sr
        sim_ref[gb] = si
        pr_in = ere_ref[...]
        pi_in = eim_ref[...]
        for j in range(L):
            pr = pwr_ref[gb, j + 1:j + 2, :]
            pi = pwi_ref[gb, j + 1:j + 2, :]
            hre_ref[sub(j), :] = hre_ref[sub(j), :] + (pr * pr_in - pi * pi_in)
            him_ref[sub(j), :] = him_ref[sub(j), :] + (pr * pi_in + pi * pr_in)
        y_ref[:, gb * cols:(gb + 1) * cols] = (_dot_bf(hre_ref[...], cre_ref[gb])
                                               + _dot_bf(him_ref[...], cim_ref[gb]))
    yv = y_ref[...] + d_ref[...] * u
    act = jax.nn.gelu(yv)
    z = act * jax.nn.sigmoid(_dot(act.astype(BF16), wglu_ref[...]))
    out = _dot(z.astype(BF16), wout_ref[...])
    hi16 = out.astype(BF16)
    lo16 = (out - hi16.astype(F32)).astype(BF16)
    out = _dot(permt_ref[...], hi16) + _dot(permt_ref[...], lo16)
    o_ref[...] = _layer_norm(ALPHA * x + out, g_ref[...], b_ref[...])


def _s5_operators(a_re, a_im, b_re, b_im, c_re, c_im, log_step):
    L = S5_SUB
    G, N, P = b_re.shape
    dt = jnp.exp(log_step)[:, None]
    mag = jnp.exp(dt * a_re)
    abar_re = mag * jnp.cos(dt * a_im)
    abar_im = mag * jnp.sin(dt * a_im)
    den = a_re * a_re + a_im * a_im
    f_re = ((abar_re - 1.0) * a_re + abar_im * a_im) / den
    f_im = (abar_im * a_re - (abar_re - 1.0) * a_im) / den
    bb_re = f_re[..., None] * b_re - f_im[..., None] * b_im
    bb_im = f_re[..., None] * b_im + f_im[..., None] * b_re
    pr, pi = [jnp.ones_like(abar_re)], [jnp.zeros_like(abar_re)]
    for _ in range(L):
        pr.append(pr[-1] * abar_re - pi[-1] * abar_im)
        pi.append(pr[-2] * abar_im + pi[-1] * abar_re)
    gbn, ngb = S5_GB, S5_NGB
    eye = jnp.eye(gbn, dtype=F32)
    blockdiag_in = lambda t: jnp.einsum('bgnp,gh->bgphn', t.reshape(ngb, gbn, N, P), eye).reshape(
        ngb, gbn * P, gbn * N)
    blockdiag_out = lambda t: jnp.einsum('bgpn,gh->bgnhp', t.reshape(ngb, gbn, P, N), eye).reshape(
        ngb, gbn * N, gbn * P)
    planes = lambda t: jnp.stack(t, 0).reshape(L + 1, ngb, gbn * N).transpose(1, 0, 2)
    return (blockdiag_in(bb_re).astype(BF16), blockdiag_in(bb_im).astype(BF16),
            blockdiag_out(c_re).astype(BF16), blockdiag_out(-c_im).astype(BF16), planes(pr), planes(pi))


def _s5_ln(x, batch, w_in, ops, d_skip, w_glu, w_out, g, b, tm=256):
    m = x.shape[0]
    seq = m // batch
    tm = min(tm, seq)
    nt = seq // tm
    nr = tm // S5_SUB
    rows = jnp.arange(tm)
    perm = (rows[None, :] == (rows[:, None] % nr) * S5_SUB + rows[:, None] // nr).astype(BF16)
    ws = [perm, perm.T, w_in, *ops, d_skip, w_glu, w_out, g, b]
    tok = pl.BlockSpec((tm, D_MODEL), lambda bi, ti: (bi * nt + ti, 0))
    wide = S5_GB * S5_STATE
    return pl.pallas_call(
        _s5_kernel,
        grid=(batch, nt),
        in_specs=[tok] + [_const_spec(w.shape) for w in ws],
        out_specs=tok,
        out_shape=jax.ShapeDtypeStruct((m, D_MODEL), F32),
        scratch_shapes=[pltpu.VMEM((tm, wide), F32), pltpu.VMEM((tm, wide), F32),
                        pltpu.VMEM((nr, wide), F32), pltpu.VMEM((nr, wide), F32),
                        pltpu.VMEM((S5_NGB, 1, wide), F32), pltpu.VMEM((S5_NGB, 1, wide), F32),
                        pltpu.VMEM((tm, D_MODEL), F32)],
        compiler_params=_cparams("parallel", "arbitrary"),
        name="s5_ln",
    )(x, *ws)


def _rw_proj_kernel(x_ref, xp_ref, mu_ref, wr_ref, wk_ref, wv_ref, w0_ref, w1_ref, w2_ref,
                    a0_ref, a1_ref, a2_ref, g1_ref, g2_ref,
                    r_ref, k_ref, v_ref, lw_ref, a_ref, g_ref, *, tiles_per_seq):
    x = x_ref[...]
    first = (pl.program_id(0) % tiles_per_seq) == 0
    prev = jnp.where(first, 0.0, xp_ref[7:8, :])
    rows = lax.broadcasted_iota(jnp.int32, x.shape, 0)
    shifted = jnp.where(rows == 0, prev, pltpu.roll(x, 1, 0))
    xx = shifted - x
    mix = lambda i: (x + xx * mu_ref[i:i + 1, :]).astype(BF16)
    lw1 = _dot(mix(1), w1_ref[...])
    la1 = _dot(mix(4), a1_ref[...])
    lg1 = _dot(mix(5), g1_ref[...])
    r_ref[...] = _dot(mix(0), wr_ref[...])
    k_ref[...] = _dot(mix(2), wk_ref[...])
    z = w0_ref[...] + _dot_bf(jnp.tanh(lw1), w2_ref[...])
    w = -_softplus(-z) - 0.5
    lw_ref[...] = -jnp.exp(w)
    a_ref[...] = jax.nn.sigmoid(a0_ref[...] + _dot_bf(la1, a2_ref[...]))
    g_ref[...] = _dot_bf(jax.nn.sigmoid(lg1), g2_ref[...])
    v_ref[...] = _dot(mix(3), wv_ref[...])


def _rw_scan_kernel(r_ref, k_ref, v_ref, lw_ref, a_ref, kk_ref, ka_ref, rk_ref, lg_ref, lb_ref,
                    mask_ref, ones_ref, bd_ref, x_ref, gate_ref, wo_ref, g_ref, b_ref, o_ref, state_ref):
    @pl.when(pl.program_id(1) == 0)
    def _():
        state_ref[...] = jnp.zeros_like(state_ref)

    L, N = RW_CHUNK, RW_HEAD
    PW = 2 * N
    tm, width = r_ref.shape
    nc, npair = tm // L, width // PW
    nt = (((1,), (1,)), ((), ()))
    tn = (((0,), (0,)), ((), ()))
    strict = mask_ref[0]
    incl = mask_ref[1]
    rr = lax.broadcasted_iota(jnp.int32, (L, PW), 0)
    cc = lax.broadcasted_iota(jnp.int32, (L, PW), 1)
    eye = jnp.where(rr == (cc & (N - 1)), 1.0, 0.0).astype(F32)
    first = cc < N
    bd = bd_ref[...]

    def bdiag(t):
        zero = jnp.zeros_like(t)
        return jnp.concatenate([jnp.where(first, t, zero), jnp.where(first, zero, t)], axis=0)

    lw = lw_ref[...]
    pos = lax.broadcasted_iota(jnp.int32, lw.shape, 0) & (L - 1)
    cum = lw
    step = 1
    while step < L:
        cum = cum + jnp.where(pos >= step, pltpu.roll(cum, step, 0), 0.0)
        step *= 2
    g_in = jnp.exp(cum)
    g_ex = jnp.exp(cum - lw)
    g_inv = jnp.exp(-cum)
    r = r_ref[...]
    k = k_ref[...]
    a = a_ref[...]

    def head_sum(t):
        hw = ones_ref.shape[0]
        hi = t.astype(BF16)
        lo = (t - hi.astype(F32)).astype(BF16)
        return jnp.concatenate(
            [_dot(hi[:, j:j + hw], ones_ref[...]) + _dot(lo[:, j:j + hw], ones_ref[...])
             for j in range(0, t.shape[1], hw)], axis=-1)

    kkr = k * kk_ref[...]
    kk_all = kkr / jnp.maximum(jnp.sqrt(head_sum(kkr * kkr)), 1e-12)
    k = k * (1.0 + (a - 1.0) * ka_ref[...])
    bonus = head_sum(r * k * rk_ref[...]) * v_ref[...]
    at_all = (-kk_all * g_ex).astype(BF16)
    bt_all = kk_all * a * g_inv
    rt_all = (r * g_in).astype(BF16)
    kt_all = k * g_inv
    vb_all = v_ref[...].astype(BF16)

    chains = [(p, c) for p in range(npair) for c in range(nc)]
    sl = lambda t, i: t[i[1] * L:(i[1] + 1) * L, i[0] * PW:(i[0] + 1) * PW]
    atc = {i: sl(at_all, i) for i in chains}
    rtc = {i: sl(rt_all, i) for i in chains}
    ktc = {i: sl(kt_all, i) for i in chains}
    btc = {i: sl(bt_all, i) for i in chains}
    vbd = {i: bdiag(sl(vb_all, i)) for i in chains}
    gl = {i: g_in[(i[1] + 1) * L - 1:(i[1] + 1) * L, i[0] * PW:(i[0] + 1) * PW] for i in chains}
    ar = {i: jnp.concatenate([atc[i], rtc[i]], axis=0) for i in chains}
    gab = {i: lax.dot_general(ar[i], bdiag(btc[i].astype(BF16)), nt, preferred_element_type=F32)
           for i in chains}
    gak = {i: lax.dot_general(ar[i], bdiag(ktc[i].astype(BF16)), nt, preferred_element_type=F32)
           for i in chains}
    a_ab = {i: gab[i][:L] * strict for i in chains}
    b_rb = {i: (gab[i][L:] * incl).astype(BF16) for i in chains}
    a_ak = {i: (gak[i][:L] * strict).astype(BF16) for i in chains}
    b_rk = {i: (gak[i][L:] * incl).astype(BF16) for i in chains}
    tinv = {i: eye + a_ab[i] * mask_ref[2] for i in chains}
    lvl, bsz = 1, 2
    while bsz < L:
        mask = mask_ref[2 + lvl]
        half = {i: _dot((a_ab[i] * mask).astype(BF16), bdiag(tinv[i].astype(BF16))) for i in chains}
        tinv = {i: tinv[i] + _dot(tinv[i].astype(BF16), bdiag(half[i].astype(BF16))) for i in chains}
        bsz *= 2
        lvl += 1
    tb = {i: tinv[i].astype(BF16) for i in chains}
    ap = {i: _dot(tb[i], bdiag(atc[i])).astype(BF16) for i in chains}
    akv = {i: _dot(a_ak[i], vbd[i]).astype(BF16) for i in chains}
    wloc = {i: _dot(tb[i], bdiag(akv[i])) for i in chains}
    bv = {i: _dot(b_rk[i], vbd[i]) for i in chains}
    bg = {i: (btc[i] * gl[i]).astype(BF16) for i in chains}
    kv = {i: lax.dot_general(sl(vb_all, i), (ktc[i] * gl[i]).astype(BF16), tn,
                             preferred_element_type=F32) * bd for i in chains}
    apr = {i: jnp.concatenate([ap[i], rtc[i]], axis=0) for i in chains}

    states = [state_ref[p] for p in range(npair)]
    outs = [[] for _ in range(npair)]
    for c in range(nc):
        both = [lax.dot_general(apr[p, c], states[p].astype(BF16), nt, preferred_element_type=F32)
                for p in range(npair)]
        ub = [(both[p][:L] + wloc[p, c]).astype(BF16) for p in range(npair)]
        for p in range(npair):
            states[p] = (states[p] * gl[p, c] + kv[p, c]
                         + lax.dot_general(ub[p], bg[p, c], tn, preferred_element_type=F32) * bd)
        for p in range(npair):
            outs[p].append(both[p][L:] + _dot(b_rb[p, c], bdiag(ub[p])) + bv[p, c])

    for p in range(npair):
        state_ref[p] = states[p]
    o = jnp.concatenate([jnp.concatenate(outs[p], axis=0) for p in range(npair)], axis=-1)
    mu = head_sum(o) * (1.0 / N)
    d = o - mu
    var = head_sum(d * d) * (1.0 / N)
    y = d * lax.rsqrt(var + RW_GN_EPS) * lg_ref[...] + lb_ref[...] + bonus
    out = _dot((y * gate_ref[...]).astype(BF16), wo_ref[...])
    o_ref[...] = _layer_norm(ALPHA * x_ref[...] + out, g_ref[...], b_ref[...])


def _rw_masks():
    L = RW_CHUNK
    i = jnp.arange(L)
    r, c = i[:, None], i[None, :]
    masks = [r > c, r >= c]
    bsz = 1
    while bsz < L:
        masks.append((r // (2 * bsz) == c // (2 * bsz)) & ((r // bsz) % 2 == 1) & ((c // bsz) % 2 == 0))
        bsz *= 2
    return jnp.tile(jnp.stack(masks, 0).astype(F32), (1, 1, 2))


def _rwkv_ln(x, batch, p, g, b, tm=512):
    m = x.shape[0]
    seq = m // batch
    tm = min(tm, seq)
    nt = seq // tm
    tok = pl.BlockSpec((tm, D_MODEL), lambda i: (i, 0))
    prev = pl.BlockSpec((8, D_MODEL), lambda i: (jnp.maximum(i * (tm // 8) - 1, 0), 0))
    names = ['mu', 'w_r', 'w_k', 'w_v', 'w0', 'w1', 'w2', 'a0', 'a1', 'a2', 'g1', 'g2']
    ws = [p[n] for n in names]
    act = jax.ShapeDtypeStruct((m, D_MODEL), F32)
    r, k, v, lw, a, gate = pl.pallas_call(
        functools.partial(_rw_proj_kernel, tiles_per_seq=nt),
        grid=(m // tm,),
        in_specs=[tok, prev] + [_const_spec(w.shape) for w in ws],
        out_specs=[tok] * 6,
        out_shape=[act] * 6,
        compiler_params=_cparams("parallel"),
        name="rwkv_proj",
    )(x, x, *ws)

    ts = min(RW_SCAN_TILE, seq)
    nts = seq // ts
    stok = pl.BlockSpec((ts, D_MODEL), lambda bi, ti: (bi * nts + ti, 0))
    masks = _rw_masks()
    head_of_lane = jnp.arange(RW_SUM_LANES) // RW_HEAD
    ones = (head_of_lane[:, None] == head_of_lane[None, :]).astype(BF16)
    bd = ones[:2 * RW_HEAD, :2 * RW_HEAD].astype(F32)
    consts = [p['k_k'], p['k_a'], p['r_k'], p['lnx_g'], p['lnx_b'], masks, ones, bd]
    tail = [p['w_o'], g, b]
    return pl.pallas_call(
        _rw_scan_kernel,
        grid=(batch, nts),
        in_specs=([stok] * 5 + [_const_spec(c.shape) for c in consts] + [stok, stok]
                  + [_const_spec(c.shape) for c in tail]),
        out_specs=stok,
        out_shape=act,
        scratch_shapes=[pltpu.VMEM((RW_HEADS // 2, 2 * RW_HEAD, 2 * RW_HEAD), F32)],
        compiler_params=_cparams("parallel", "arbitrary"),
        name="rwkv_scan_ln",
    )(r, k, v, lw, a, *consts, x, gate, *tail)


def _lru_kernel(x_ref, win_ref, cw_ref, cb_ref, wa_ref, ba_ref, wx_ref, bx_ref, lam_ref, wout_ref,
                g_ref, b_ref, o_ref, xr_ref, h_ref):
    W = D_MODEL
    tm = x_ref.shape[0]

    @pl.when(pl.program_id(1) == 0)
    def _():
        xr_ref[...] = jnp.zeros_like(xr_ref)
        h_ref[...] = jnp.zeros_like(h_ref)

    x = x_ref[...]
    proj = _dot(x.astype(BF16), win_ref[...])
    gate = jax.nn.gelu(proj[:, :W])
    xr_ref[0:8, :] = xr_ref[tm:tm + 8, :]
    xr_ref[8:tm + 8, :] = proj[:, W:]
    xc = cb_ref[...] + cw_ref[3:4, :] * xr_ref[8:tm + 8, :]
    for j in range(CONV_WIDTH - 1):
        xc = xc + cw_ref[j:j + 1, :] * xr_ref[pl.ds(5 + j, tm), :]
    xcb = xc.astype(BF16)
    gr = jnp.concatenate([_dot(xcb[:, kb * LRU_BLOCK:(kb + 1) * LRU_BLOCK], wa_ref[kb])
                          for kb in range(LRU_BLOCKS)], axis=-1) + ba_ref[...]
    gi = jnp.concatenate([_dot(xcb[:, kb * LRU_BLOCK:(kb + 1) * LRU_BLOCK], wx_ref[kb])
                          for kb in range(LRU_BLOCKS)], axis=-1) + bx_ref[...]
    log_a = -LRU_C * jax.nn.sigmoid(gr) * _softplus(-lam_ref[...])
    av = jnp.exp(log_a)
    bv = jnp.sqrt(1.0 - av * av) * (jax.nn.sigmoid(gi) * xc)
    pos = lax.broadcasted_iota(jnp.int32, av.shape, 0) & (LRU_ROWS - 1)
    step = 1
    while step < LRU_ROWS:
        keep = pos >= step
        bv = av * jnp.where(keep, pltpu.roll(bv, step, 0), 0.0) + bv
        av = av * jnp.where(keep, pltpu.roll(av, step, 0), 1.0)
        step *= 2
    nb = tm // LRU_ROWS
    a_end = av.reshape(nb, LRU_ROWS, W)[:, LRU_ROWS - 1, :]
    b_end = bv.reshape(nb, LRU_ROWS, W)[:, LRU_ROWS - 1, :]
    h = h_ref[...]
    carries = []
    for i in range(nb):
        carries.append(h)
        h = a_end[i:i + 1] * h + b_end[i:i + 1]
    h_ref[...] = h
    carry = jnp.concatenate(carries, axis=0)
    hs = av * jnp.broadcast_to(carry[:, None, :], (nb, LRU_ROWS, W)).reshape(tm, W) + bv
    y = _dot((hs * gate).astype(BF16), wout_ref[...])
    o_ref[...] = _layer_norm(ALPHA * x + y, g_ref[...], b_ref[...])


def _lru_ln(x, batch, p, g, b, tm=512):
    m = x.shape[0]
    seq = m // batch
    tm = min(tm, seq)
    nt = seq // tm
    tok = pl.BlockSpec((tm, D_MODEL), lambda bi, ti: (bi * nt + ti, 0))
    names = ['w_in', 'conv_w', 'conv_b', 'w_a', 'b_a', 'w_x', 'b_x', 'lam', 'w_out']
    ws = [p[n] for n in names]
    return pl.pallas_call(
        _lru_kernel,
        grid=(batch, nt),
        in_specs=[tok] + [_const_spec(w.shape) for w in ws] + [_const_spec(g.shape), _const_spec(b.shape)],
        out_specs=tok,
        out_shape=jax.ShapeDtypeStruct((m, D_MODEL), F32),
        scratch_shapes=[pltpu.VMEM((tm + 8, D_MODEL), F32), pltpu.VMEM((1, D_MODEL), F32)],
        compiler_params=_cparams("parallel", "arbitrary"),
        name="rglru_ln",
    )(x, *ws, g, b)


def kernel(x, ln_g, ln_b, ffn_w1, ffn_w3, ffn_w2, ret_w_in, ret_w_out, s5_w_in, s5_a_re, s5_a_im, s5_b_re, s5_b_im, s5_c_re, s5_c_im, s5_d, s5_log_step, s5_w_glu, s5_w_out, rw_mu, rw_w_r, rw_w_k, rw_w_v, rw_w0, rw_w1, rw_w2, rw_a0, rw_a1, rw_a2, rw_g1, rw_g2, rw_k_k, rw_k_a, rw_r_k, rw_lnx_g, rw_lnx_b, rw_w_o, lru_w_in, lru_conv_w, lru_conv_b, lru_w_a, lru_b_a, lru_w_x, lru_b_x, lru_lambda, lru_w_out):
    batch, seq, d = x.shape
    depth = ln_g.shape[0]
    h = x.reshape(batch * seq, d)
    bf = lambda t: t.astype(BF16)
    row = lambda t: t.reshape(1, -1)
    w1, w3, w2 = bf(ffn_w1), bf(ffn_w3), bf(ffn_w2)
    for i in range(depth):
        m, j = i % N_MIXERS, i // N_MIXERS
        h = _ffn_ln(h, w1, w3, w2, i, 0, row(ln_g[i, 0]), row(ln_b[i, 0]))
        g, b = row(ln_g[i, 1]), row(ln_b[i, 1])
        if m == 0:
            h = _retention_ln(h, batch, bf(ret_w_in[j]), bf(ret_w_out[j]), g, b)
        elif m == 1:
            ops = _s5_operators(s5_a_re[j], s5_a_im[j], s5_b_re[j], s5_b_im[j], s5_c_re[j], s5_c_im[j],
                                s5_log_step[j])
            h = _s5_ln(h, batch, bf(s5_w_in[j]), ops, row(s5_d[j]), bf(s5_w_glu[j]), bf(s5_w_out[j]), g, b)
        elif m == 2:
            p = dict(mu=rw_mu[j], w_r=bf(rw_w_r[j]), w_k=bf(rw_w_k[j]), w_v=bf(rw_w_v[j]),
                     w0=row(rw_w0[j]), w1=bf(rw_w1[j]), w2=bf(rw_w2[j]),
                     a0=row(rw_a0[j]), a1=bf(rw_a1[j]), a2=bf(rw_a2[j]), g1=bf(rw_g1[j]), g2=bf(rw_g2[j]),
                     k_k=row(rw_k_k[j]), k_a=row(rw_k_a[j]), r_k=row(rw_r_k[j]),
                     lnx_g=row(rw_lnx_g[j]), lnx_b=row(rw_lnx_b[j]), w_o=bf(rw_w_o[j]))
            h = _rwkv_ln(h, batch, p, g, b)
        else:
            p = dict(w_in=bf(lru_w_in[j]), conv_w=lru_conv_w[j], conv_b=row(lru_conv_b[j]),
                     w_a=bf(lru_w_a[j]), b_a=row(lru_b_a[j]), w_x=bf(lru_w_x[j]), b_x=row(lru_b_x[j]),
                     lam=row(lru_lambda[j]), w_out=bf(lru_w_out[j]))
            h = _lru_ln(h, batch, p, g, b)
        h = _ffn_ln(h, w1, w3, w2, i, 1, row(ln_g[i, 2]), row(ln_b[i, 2]))
    return h.reshape(batch, seq, d)
```

```python
import functools
import math

import jax
import jax.numpy as jnp
from jax import lax
from jax.experimental import pallas as pl
from jax.experimental.pallas import tpu as pltpu

F32 = jnp.float32
BF16 = jnp.bfloat16

D_MODEL = 1024
DEPTH = 4
N_MIXERS = 4
ALPHA = (2 * DEPTH) ** 0.25
LN_EPS = 1e-5
D_FF = 2816
FFN_HALF = 0.5

RET_HEADS = 4
RET_DK = 256
RET_DV = 512
ROPE_BASE = 10000.0
RET_GN_EPS = 1e-5

S5_GROUPS = 64
S5_GROUP = 16
S5_STATE = 64
S5_SUB = 16
S5_GB = 16
S5_NGB = S5_GROUPS // S5_GB

RW_HEADS = 16
RW_HEAD = 64
RW_GN_EPS = 64e-5
RW_CHUNK = 64
RW_SUM_LANES = 256
RW_SCAN_TILE = 256

LRU_BLOCKS = 4
LRU_BLOCK = 256
CONV_WIDTH = 4
LRU_C = 8.0
LRU_ROWS = 8

VMEM_LIMIT = 56 * 1024 * 1024


def _cparams(*sem):
    return pltpu.CompilerParams(dimension_semantics=sem, vmem_limit_bytes=VMEM_LIMIT)


def _const_spec(shape):
    nd = len(shape)
    return pl.BlockSpec(shape, lambda *_: (0,) * nd, pipeline_mode=pl.Buffered(1))


def _layer_norm(r, g, b):
    mu = jnp.mean(r, axis=-1, keepdims=True)
    d = r - mu
    var = jnp.mean(d * d, axis=-1, keepdims=True)
    return d * lax.rsqrt(var + LN_EPS) * g + b


def _dot(a, b):
    return jnp.dot(a, b, preferred_element_type=F32)


def _dot_bf(a, b):
    return jnp.dot(a.astype(BF16), b.astype(BF16), preferred_element_type=F32)


def _softplus(z):
    return jnp.maximum(z, 0.0) + jnp.log(1.0 + jnp.exp(-jnp.abs(z)))


def _ffn_kernel(x_ref, w1_ref, w3_ref, w2_ref, g_ref, b_ref, o_ref):
    x = x_ref[...]
    xb = x.astype(BF16)
    h1 = _dot(xb, w1_ref[...])
    h3 = _dot(xb, w3_ref[...])
    act = (h1 * jax.nn.sigmoid(h1) * h3).astype(BF16)
    y = _dot(act, w2_ref[...])
    o_ref[...] = _layer_norm(ALPHA * x + FFN_HALF * y, g_ref[...], b_ref[...])


def _ffn_ln(x, w1, w3, w2, layer, slot, g, b, tm=512):
    m = x.shape[0]
    tm = min(tm, m)
    tok = pl.BlockSpec((tm, D_MODEL), lambda i: (i, 0))
    pick = lambda w: pl.BlockSpec((None, None) + w.shape[2:], lambda i: (layer, slot, 0, 0),
                                  pipeline_mode=pl.Buffered(1))
    return pl.pallas_call(
        _ffn_kernel,
        grid=(m // tm,),
        in_specs=[tok, pick(w1), pick(w3), pick(w2), _const_spec(g.shape), _const_spec(b.shape)],
        out_specs=tok,
        out_shape=jax.ShapeDtypeStruct((m, D_MODEL), F32),
        compiler_params=_cparams("parallel"),
        name="ffn_ln",
    )(x, w1, w3, w2, g, b)


def _rotate(t1, t2, cos, sin):
    return jnp.concatenate([t1 * cos - t2 * sin, t1 * sin + t2 * cos], axis=-1)


def _ret_kernel(x_ref, cos_ref, sin_ref, win_ref, wout_ref, dmat_ref, qd_ref, kd_ref,
                g_ref, b_ref, o_ref, state_ref):
    @pl.when(pl.program_id(1) == 0)
    def _():
        state_ref[...] = jnp.zeros_like(state_ref)

    H, DK, DV = RET_HEADS, RET_DK, RET_DV
    half = DK // 2
    x = x_ref[...]
    tc = x.shape[0]
    proj = _dot(x.astype(BF16), win_ref[...])
    cos = cos_ref[...]
    sin = sin_ref[...]
    y = jnp.zeros_like(x)
    for h in range(H):
        q0 = h * DK
        k0 = H * DK + h * DK
        v0 = 2 * H * DK + h * DV
        g0 = 2 * H * DK + H * DV + h * DV
        qh = _rotate(proj[:, q0:q0 + half], proj[:, q0 + half:q0 + DK], cos, sin)
        kh = _rotate(proj[:, k0:k0 + half], proj[:, k0 + half:k0 + DK], cos, sin) * (DK ** -0.5)
        vh = proj[:, v0:v0 + DV].astype(BF16)
        gate = proj[:, g0:g0 + DV]
        s = lax.dot_general(qh.astype(BF16), kh.astype(BF16), (((1,), (1,)), ((), ())),
                            preferred_element_type=F32) * dmat_ref[h]
        state = state_ref[h]
        o = _dot(s.astype(BF16), vh) + _dot_bf(qh * qd_ref[h], state)
        kdec = (kh * kd_ref[h]).T.astype(BF16)
        chunk_decay = math.exp(tc * math.log1p(-(2.0 ** (-5.0 - h))))
        state_ref[h] = state * chunk_decay + _dot(kdec, vh)
        mu = jnp.mean(o, axis=-1, keepdims=True)
        d = o - mu
        var = jnp.mean(d * d, axis=-1, keepdims=True)
        o = d * lax.rsqrt(var + RET_GN_EPS)
        o = gate * jax.nn.sigmoid(gate) * o
        y = y + _dot(o.astype(BF16), wout_ref[h * DV:(h + 1) * DV, :])
    o_ref[...] = _layer_norm(ALPHA * x + y, g_ref[...], b_ref[...])


def _retention_ln(x, batch, w_in, w_out, g, b, tc=512):
    m = x.shape[0]
    seq = m // batch
    tc = min(tc, seq)
    nt = seq // tc
    H, DK = RET_HEADS, RET_DK
    half = DK // 2
    inv = ROPE_BASE ** (-jnp.arange(half, dtype=F32) / half)
    ang = jnp.arange(seq, dtype=F32)[:, None] * inv[None, :]
    cos, sin = jnp.cos(ang), jnp.sin(ang)
    log_gamma = jnp.log1p(-jnp.power(2.0, -5.0 - jnp.arange(H, dtype=F32)))
    pos = jnp.arange(tc, dtype=F32)
    rel = pos[:, None] - pos[None, :]
    dmat = jnp.where(rel >= 0, jnp.exp(jnp.maximum(rel, 0.0)[None] * log_gamma[:, None, None]), 0.0)
    qd = jnp.broadcast_to(jnp.exp((pos + 1.0)[None, :, None] * log_gamma[:, None, None]), (H, tc, DK))
    kd = jnp.broadcast_to(jnp.exp((tc - 1.0 - pos)[None, :, None] * log_gamma[:, None, None]), (H, tc, DK))

    tok = pl.BlockSpec((tc, D_MODEL), lambda bi, ti: (bi * nt + ti, 0))
    rope = pl.BlockSpec((tc, half), lambda bi, ti: (ti, 0))
    return pl.pallas_call(
        _ret_kernel,
        grid=(batch, nt),
        in_specs=[tok, rope, rope, _const_spec(w_in.shape), _const_spec(w_out.shape),
                  _const_spec(dmat.shape), _const_spec(qd.shape), _const_spec(kd.shape),
                  _const_spec(g.shape), _const_spec(b.shape)],
        out_specs=tok,
        out_shape=jax.ShapeDtypeStruct((m, D_MODEL), F32),
        scratch_shapes=[pltpu.VMEM((H, DK, RET_DV), F32)],
        compiler_params=_cparams("parallel", "arbitrary"),
        name="retention_ln",
    )(x, cos, sin, w_in, w_out, dmat, qd, kd, g, b)


def _s5_kernel(x_ref, perm_ref, permt_ref, win_ref, bre_ref, bim_ref, cre_ref, cim_ref, pwr_ref, pwi_ref, d_ref,
               wglu_ref, wout_ref, g_ref, b_ref, o_ref, hre_ref, him_ref, ere_ref, eim_ref,
               sre_ref, sim_ref, y_ref):
    @pl.when(pl.program_id(1) == 0)
    def _():
        sre_ref[...] = jnp.zeros_like(sre_ref)
        sim_ref[...] = jnp.zeros_like(sim_ref)

    L = S5_SUB
    tm = x_ref.shape[0]
    nr = tm // L
    cols = S5_GB * S5_GROUP
    x = x_ref[...]
    xp = _dot(perm_ref[...], x.astype(BF16)).astype(BF16)
    u = _dot(xp, win_ref[...])
    ub = u.astype(BF16)
    for gb in range(S5_NGB):
        ug = ub[:, gb * cols:(gb + 1) * cols]
        hre_ref[...] = _dot(ug, bre_ref[gb])
        him_ref[...] = _dot(ug, bim_ref[gb])
        sub = lambda j: slice(j * nr, (j + 1) * nr)
        ar = pwr_ref[gb, 1:2, :]
        ai = pwi_ref[gb, 1:2, :]
        hr = hre_ref[sub(0), :]
        hi = him_ref[sub(0), :]
        for j in range(1, L):
            hr, hi = (ar * hr - ai * hi + hre_ref[sub(j), :], ar * hi + ai * hr + him_ref[sub(j), :])
            hre_ref[sub(j), :] = hr
            him_ref[sub(j), :] = hi
        ere_ref[...] = hr
        eim_ref[...] = hi
        alr = pwr_ref[gb, L:L + 1, :]
        ali = pwi_ref[gb, L:L + 1, :]

        def row(c, carry):
            sr, si = carry
            er = ere_ref[pl.ds(c, 1), :]
            ei = eim_ref[pl.ds(c, 1), :]
            ere_ref[pl.ds(c, 1), :] = sr
            eim_ref[pl.ds(c, 1), :] = si
            return alr * sr - ali * si + er, alr * si + ali * sr + ei

        sr, si = lax.fori_loop(0, nr, row, (sre_ref[gb], sim_ref[gb]))
        sre_ref[gb] = sr
        sim_ref[gb] = si
        pr_in = ere_ref[...]
        pi_in = eim_ref[...]
        for j in range(L):
            pr = pwr_ref[gb, j + 1:j + 2, :]
            pi = pwi_ref[gb, j + 1:j + 2, :]
            hre_ref[sub(j), :] = hre_ref[sub(j), :] + (pr * pr_in - pi * pi_in)
            him_ref[sub(j), :] = him_ref[sub(j), :] + (pr * pi_in + pi * pr_in)
        y_ref[:, gb * cols:(gb + 1) * cols] = (_dot_bf(hre_ref[...], cre_ref[gb])
                                               + _dot_bf(him_ref[...], cim_ref[gb]))
    yv = y_ref[...] + d_ref[...] * u
    act = jax.nn.gelu(yv)
    z = act * jax.nn.sigmoid(_dot(act.astype(BF16), wglu_ref[...]))
    out = _dot(z.astype(BF16), wout_ref[...])
    hi16 = out.astype(BF16)
    lo16 = (out - hi16.astype(F32)).astype(BF16)
    out = _dot(permt_ref[...], hi16) + _dot(permt_ref[...], lo16)
    o_ref[...] = _layer_norm(ALPHA * x + out, g_ref[...], b_ref[...])


def _s5_zoh_kernel(are_ref, aim_ref, ls_ref, bre_ref, bim_ref, bbre_ref, bbim_ref, pwr_ref, pwi_ref):
    a_re = are_ref[...]
    a_im = aim_ref[...]
    dt = jnp.exp(ls_ref[...])
    mag = jnp.exp(dt * a_re)
    abar_re = mag * jnp.cos(dt * a_im)
    abar_im = mag * jnp.sin(dt * a_im)
    den = a_re * a_re + a_im * a_im
    f_re = (((abar_re - 1.0) * a_re + abar_im * a_im) / den)[:, None, :]
    f_im = ((abar_im * a_re - (abar_re - 1.0) * a_im) / den)[:, None, :]
    b_re = bre_ref[...]
    b_im = bim_ref[...]
    bbre_ref[...] = f_re * b_re - f_im * b_im
    bbim_ref[...] = f_re * b_im + f_im * b_re
    pr = jnp.ones_like(abar_re)
    pi = jnp.zeros_like(abar_re)
    pwr_ref[0] = pr
    pwi_ref[0] = pi
    for l in range(1, S5_SUB + 1):
        pr, pi = pr * abar_re - pi * abar_im, pr * abar_im + pi * abar_re
        pwr_ref[l] = pr
        pwi_ref[l] = pi


def _s5_operators(a_re, a_im, b_re, b_im, c_re, c_im, log_step):
    L = S5_SUB
    G, N, P = b_re.shape
    vm = pl.BlockSpec(memory_space=pltpu.VMEM)
    bb_re, bb_im, pw_re, pw_im = pl.pallas_call(
        _s5_zoh_kernel,
        in_specs=[vm] * 5,
        out_specs=[vm] * 4,
        out_shape=[jax.ShapeDtypeStruct((G, P, N), F32)] * 2 + [jax.ShapeDtypeStruct((L + 1, G, N), F32)] * 2,
        name="s5_zoh",
    )(a_re, a_im, log_step.reshape(G, 1), b_re.transpose(0, 2, 1), b_im.transpose(0, 2, 1))
    gbn, ngb = S5_GB, S5_NGB
    eye = jnp.eye(gbn, dtype=F32)
    blockdiag_in = lambda t: jnp.einsum('bgpn,gh->bgphn', t.reshape(ngb, gbn, P, N), eye).reshape(
        ngb, gbn * P, gbn * N)
    blockdiag_out = lambda t: jnp.einsum('bgpn,gh->bgnhp', t.reshape(ngb, gbn, P, N), eye).reshape(
        ngb, gbn * N, gbn * P)
    planes = lambda t: t.reshape(L + 1, ngb, gbn * N).transpose(1, 0, 2)
    return (blockdiag_in(bb_re).astype(BF16), blockdiag_in(bb_im).astype(BF16),
            blockdiag_out(c_re).astype(BF16), blockdiag_out(-c_im).astype(BF16), planes(pw_re), planes(pw_im))


def _s5_ln(x, batch, w_in, ops, d_skip, w_glu, w_out, g, b, tm=512):
    m = x.shape[0]
    seq = m // batch
    tm = min(tm, seq)
    nt = seq // tm
    nr = tm // S5_SUB
    rows = jnp.arange(tm)
    perm = (rows[None, :] == (rows[:, None] % nr) * S5_SUB + rows[:, None] // nr).astype(BF16)
    ws = [perm, perm.T, w_in, *ops, d_skip, w_glu, w_out, g, b]
    tok = pl.BlockSpec((tm, D_MODEL), lambda bi, ti: (bi * nt + ti, 0))
    wide = S5_GB * S5_STATE
    return pl.pallas_call(
        _s5_kernel,
        grid=(batch, nt),
        in_specs=[tok] + [_const_spec(w.shape) for w in ws],
        out_specs=tok,
        out_shape=jax.ShapeDtypeStruct((m, D_MODEL), F32),
        scratch_shapes=[pltpu.VMEM((tm, wide), F32), pltpu.VMEM((tm, wide), F32),
                        pltpu.VMEM((nr, wide), F32), pltpu.VMEM((nr, wide), F32),
                        pltpu.VMEM((S5_NGB, 1, wide), F32), pltpu.VMEM((S5_NGB, 1, wide), F32),
                        pltpu.VMEM((tm, D_MODEL), F32)],
        compiler_params=_cparams("parallel", "arbitrary"),
        name="s5_ln",
    )(x, *ws)


def _rw_proj_kernel(x_ref, xp_ref, mu_ref, wr_ref, wk_ref, wv_ref, w0_ref, w1_ref, w2_ref,
                    a0_ref, a1_ref, a2_ref, g1_ref, g2_ref,
                    r_ref, k_ref, v_ref, lw_ref, a_ref, g_ref, *, tiles_per_seq):
    x = x_ref[...]
    first = (pl.program_id(0) % tiles_per_seq) == 0
    prev = jnp.where(first, 0.0, xp_ref[7:8, :])
    rows = lax.broadcasted_iota(jnp.int32, x.shape, 0)
    shifted = jnp.where(rows == 0, prev, pltpu.roll(x, 1, 0))
    xx = shifted - x
    mix = lambda i: (x + xx * mu_ref[i:i + 1, :]).astype(BF16)
    lw1 = _dot(mix(1), w1_ref[...])
    la1 = _dot(mix(4), a1_ref[...])
    lg1 = _dot(mix(5), g1_ref[...])
    r_ref[...] = _dot(mix(0), wr_ref[...])
    k_ref[...] = _dot(mix(2), wk_ref[...])
    z = w0_ref[...] + _dot_bf(jnp.tanh(lw1), w2_ref[...])
    w = -_softplus(-z) - 0.5
    lw_ref[...] = -jnp.exp(w)
    a_ref[...] = jax.nn.sigmoid(a0_ref[...] + _dot_bf(la1, a2_ref[...]))
    g_ref[...] = _dot_bf(jax.nn.sigmoid(lg1), g2_ref[...])
    v_ref[...] = _dot(mix(3), wv_ref[...])


def _rw_scan_kernel(r_ref, k_ref, v_ref, lw_ref, a_ref, kk_ref, ka_ref, rk_ref, lg_ref, lb_ref,
                    mask_ref, ones_ref, bd_ref, x_ref, gate_ref, wo_ref, g_ref, b_ref, o_ref, state_ref):
    @pl.when(pl.program_id(1) == 0)
    def _():
        state_ref[...] = jnp.zeros_like(state_ref)

    L, N = RW_CHUNK, RW_HEAD
    PW = 2 * N
    tm, width = r_ref.shape
    nc, npair = tm // L, width // PW
    nt = (((1,), (1,)), ((), ()))
    tn = (((0,), (0,)), ((), ()))
    strict = mask_ref[0]
    incl = mask_ref[1]
    rr = lax.broadcasted_iota(jnp.int32, (L, PW), 0)
    cc = lax.broadcasted_iota(jnp.int32, (L, PW), 1)
    eye = jnp.where(rr == (cc & (N - 1)), 1.0, 0.0).astype(F32)
    first = cc < N
    bd = bd_ref[...]

    def bdiag(t):
        zero = jnp.zeros_like(t)
        return jnp.concatenate([jnp.where(first, t, zero), jnp.where(first, zero, t)], axis=0)

    lw = lw_ref[...]
    pos = lax.broadcasted_iota(jnp.int32, lw.shape, 0) & (L - 1)
    cum = lw
    step = 1
    while step < L:
        cum = cum + jnp.where(pos >= step, pltpu.roll(cum, step, 0), 0.0)
        step *= 2
    g_in = jnp.exp(cum)
    g_ex = jnp.exp(cum - lw)
    g_inv = jnp.exp(-cum)
    r = r_ref[...]
    k = k_ref[...]
    a = a_ref[...]

    def head_sum(t):
        hw = ones_ref.shape[0]
        hi = t.astype(BF16)
        lo = (t - hi.astype(F32)).astype(BF16)
        return jnp.concatenate(
            [_dot(hi[:, j:j + hw], ones_ref[...]) + _dot(lo[:, j:j + hw], ones_ref[...])
             for j in range(0, t.shape[1], hw)], axis=-1)

    kkr = k * kk_ref[...]
    kk_all = kkr / jnp.maximum(jnp.sqrt(head_sum(kkr * kkr)), 1e-12)
    k = k * (1.0 + (a - 1.0) * ka_ref[...])
    bonus = head_sum(r * k * rk_ref[...]) * v_ref[...]
    at_all = (-kk_all * g_ex).astype(BF16)
    bt_all = kk_all * a * g_inv
    rt_all = (r * g_in).astype(BF16)
    kt_all = k * g_inv
    vb_all = v_ref[...].astype(BF16)

    chains = [(p, c) for p in range(npair) for c in range(nc)]
    sl = lambda t, i: t[i[1] * L:(i[1] + 1) * L, i[0] * PW:(i[0] + 1) * PW]
    atc = {i: sl(at_all, i) for i in chains}
    rtc = {i: sl(rt_all, i) for i in chains}
    ktc = {i: sl(kt_all, i) for i in chains}
    btc = {i: sl(bt_all, i) for i in chains}
    vbd = {i: bdiag(sl(vb_all, i)) for i in chains}
    gl = {i: g_in[(i[1] + 1) * L - 1:(i[1] + 1) * L, i[0] * PW:(i[0] + 1) * PW] for i in chains}
    ar = {i: jnp.concatenate([atc[i], rtc[i]], axis=0) for i in chains}
    gab = {i: lax.dot_general(ar[i], bdiag(btc[i].astype(BF16)), nt, preferred_element_type=F32)
           for i in chains}
    gak = {i: lax.dot_general(ar[i], bdiag(ktc[i].astype(BF16)), nt, preferred_element_type=F32)
           for i in chains}
    a_ab = {i: gab[i][:L] * strict for i in chains}
    b_rb = {i: (gab[i][L:] * incl).astype(BF16) for i in chains}
    a_ak = {i: (gak[i][:L] * strict).astype(BF16) for i in chains}
    b_rk = {i: (gak[i][L:] * incl).astype(BF16) for i in chains}
    tinv = {i: eye + a_ab[i] * mask_ref[2] for i in chains}
    lvl, bsz = 1, 2
    while bsz < L:
        mask = mask_ref[2 + lvl]
        half = {i: _dot((a_ab[i] * mask).astype(BF16), bdiag(tinv[i].astype(BF16))) for i in chains}
        tinv = {i: tinv[i] + _dot(tinv[i].astype(BF16), bdiag(half[i].astype(BF16))) for i in chains}
        bsz *= 2
        lvl += 1
    tb = {i: tinv[i].astype(BF16) for i in chains}
    ap = {i: _dot(tb[i], bdiag(atc[i])).astype(BF16) for i in chains}
    akv = {i: _dot(a_ak[i], vbd[i]).astype(BF16) for i in chains}
    wloc = {i: _dot(tb[i], bdiag(akv[i])) for i in chains}
    bv = {i: _dot(b_rk[i], vbd[i]) for i in chains}
    bg = {i: (btc[i] * gl[i]).astype(BF16) for i in chains}
    kv = {i: lax.dot_general(sl(vb_all, i), (ktc[i] * gl[i]).astype(BF16), tn,
                             preferred_element_type=F32) * bd for i in chains}
    apr = {i: jnp.concatenate([ap[i], rtc[i]], axis=0) for i in chains}

    states = [state_ref[p] for p in range(npair)]
    outs = [[] for _ in range(npair)]
    for c in range(nc):
        both = [lax.dot_general(apr[p, c], states[p].astype(BF16), nt, preferred_element_type=F32)
                for p in range(npair)]
        ub = [(both[p][:L] + wloc[p, c]).astype(BF16) for p in range(npair)]
        for p in range(npair):
            states[p] = (states[p] * gl[p, c] + kv[p, c]
                         + lax.dot_general(ub[p], bg[p, c], tn, preferred_element_type=F32) * bd)
        for p in range(npair):
            outs[p].append(both[p][L:] + _dot(b_rb[p, c], bdiag(ub[p])) + bv[p, c])

    for p in range(npair):
        state_ref[p] = states[p]
    o = jnp.concatenate([jnp.concatenate(outs[p], axis=0) for p in range(npair)], axis=-1)
    mu = head_sum(o) * (1.0 / N)
    d = o - mu
    var = head_sum(d * d) * (1.0 / N)
    y = d * lax.rsqrt(var + RW_GN_EPS) * lg_ref[...] + lb_ref[...] + bonus
    out = _dot((y * gate_ref[...]).astype(BF16), wo_ref[...])
    o_ref[...] = _layer_norm(ALPHA * x_ref[...] + out, g_ref[...], b_ref[...])


def _rw_masks():
    L = RW_CHUNK
    i = jnp.arange(L)
    r, c = i[:, None], i[None, :]
    masks = [r > c, r >= c]
    bsz = 1
    while bsz < L:
        masks.append((r // (2 * bsz) == c // (2 * bsz)) & ((r // bsz) % 2 == 1) & ((c // bsz) % 2 == 0))
        bsz *= 2
    return jnp.tile(jnp.stack(masks, 0).astype(F32), (1, 1, 2))


def _rwkv_ln(x, batch, p, g, b, tm=512):
    m = x.shape[0]
    seq = m // batch
    tm = min(tm, seq)
    nt = seq // tm
    tok = pl.BlockSpec((tm, D_MODEL), lambda i: (i, 0))
    prev = pl.BlockSpec((8, D_MODEL), lambda i: (jnp.maximum(i * (tm // 8) - 1, 0), 0))
    names = ['mu', 'w_r', 'w_k', 'w_v', 'w0', 'w1', 'w2', 'a0', 'a1', 'a2', 'g1', 'g2']
    ws = [p[n] for n in names]
    act = jax.ShapeDtypeStruct((m, D_MODEL), F32)
    r, k, v, lw, a, gate = pl.pallas_call(
        functools.partial(_rw_proj_kernel, tiles_per_seq=nt),
        grid=(m // tm,),
        in_specs=[tok, prev] + [_const_spec(w.shape) for w in ws],
        out_specs=[tok] * 6,
        out_shape=[act] * 6,
        compiler_params=_cparams("parallel"),
        name="rwkv_proj",
    )(x, x, *ws)

    ts = min(RW_SCAN_TILE, seq)
    nts = seq // ts
    stok = pl.BlockSpec((ts, D_MODEL), lambda bi, ti: (bi * nts + ti, 0))
    masks = _rw_masks()
    head_of_lane = jnp.arange(RW_SUM_LANES) // RW_HEAD
    ones = (head_of_lane[:, None] == head_of_lane[None, :]).astype(BF16)
    bd = ones[:2 * RW_HEAD, :2 * RW_HEAD].astype(F32)
    consts = [p['k_k'], p['k_a'], p['r_k'], p['lnx_g'], p['lnx_b'], masks, ones, bd]
    tail = [p['w_o'], g, b]
    return pl.pallas_call(
        _rw_scan_kernel,
        grid=(batch, nts),
        in_specs=([stok] * 5 + [_const_spec(c.shape) for c in consts] + [stok, stok]
                  + [_const_spec(c.shape) for c in tail]),
        out_specs=stok,
        out_shape=act,
        scratch_shapes=[pltpu.VMEM((RW_HEADS // 2, 2 * RW_HEAD, 2 * RW_HEAD), F32)],
        compiler_params=_cparams("parallel", "arbitrary"),
        name="rwkv_scan_ln",
    )(r, k, v, lw, a, *consts, x, gate, *tail)


def _lru_kernel(x_ref, win_ref, cw_ref, cb_ref, wa_ref, ba_ref, wx_ref, bx_ref, lam_ref, wout_ref,
                g_ref, b_ref, o_ref, xr_ref, h_ref):
    W = D_MODEL
    tm = x_ref.shape[0]

    @pl.when(pl.program_id(1) == 0)
    def _():
        xr_ref[...] = jnp.zeros_like(xr_ref)
        h_ref[...] = jnp.zeros_like(h_ref)

    x = x_ref[...]
    proj = _dot(x.astype(BF16), win_ref[...])
    gate = jax.nn.gelu(proj[:, :W])
    xr_ref[0:8, :] = xr_ref[tm:tm + 8, :]
    xr_ref[8:tm + 8, :] = proj[:, W:]
    xc = cb_ref[...] + cw_ref[3:4, :] * xr_ref[8:tm + 8, :]
    for j in range(CONV_WIDTH - 1):
        xc = xc + cw_ref[j:j + 1, :] * xr_ref[pl.ds(5 + j, tm), :]
    xcb = xc.astype(BF16)
    gr = jnp.concatenate([_dot(xcb[:, kb * LRU_BLOCK:(kb + 1) * LRU_BLOCK], wa_ref[kb])
                          for kb in range(LRU_BLOCKS)], axis=-1) + ba_ref[...]
    gi = jnp.concatenate([_dot(xcb[:, kb * LRU_BLOCK:(kb + 1) * LRU_BLOCK], wx_ref[kb])
                          for kb in range(LRU_BLOCKS)], axis=-1) + bx_ref[...]
    log_a = -LRU_C * jax.nn.sigmoid(gr) * _softplus(-lam_ref[...])
    av = jnp.exp(log_a)
    bv = jnp.sqrt(1.0 - av * av) * (jax.nn.sigmoid(gi) * xc)
    pos = lax.broadcasted_iota(jnp.int32, av.shape, 0) & (LRU_ROWS - 1)
    step = 1
    while step < LRU_ROWS:
        keep = pos >= step
        bv = av * jnp.where(keep, pltpu.roll(bv, step, 0), 0.0) + bv
        av = av * jnp.where(keep, pltpu.roll(av, step, 0), 1.0)
        step *= 2
    nb = tm // LRU_ROWS
    a_end = av.reshape(nb, LRU_ROWS, W)[:, LRU_ROWS - 1, :]
    b_end = bv.reshape(nb, LRU_ROWS, W)[:, LRU_ROWS - 1, :]
    h = h_ref[...]
    carries = []
    for i in range(nb):
        carries.append(h)
        h = a_end[i:i + 1] * h + b_end[i:i + 1]
    h_ref[...] = h
    carry = jnp.concatenate(carries, axis=0)
    hs = av * jnp.broadcast_to(carry[:, None, :], (nb, LRU_ROWS, W)).reshape(tm, W) + bv
    y = _dot((hs * gate).astype(BF16), wout_ref[...])
    o_ref[...] = _layer_norm(ALPHA * x + y, g_ref[...], b_ref[...])


def _lru_ln(x, batch, p, g, b, tm=512):
    m = x.shape[0]
    seq = m // batch
    tm = min(tm, seq)
    nt = seq // tm
    tok = pl.BlockSpec((tm, D_MODEL), lambda bi, ti: (bi * nt + ti, 0))
    names = ['w_in', 'conv_w', 'conv_b', 'w_a', 'b_a', 'w_x', 'b_x', 'lam', 'w_out']
    ws = [p[n] for n in names]
    return pl.pallas_call(
        _lru_kernel,
        grid=(batch, nt),
        in_specs=[tok] + [_const_spec(w.shape) for w in ws] + [_const_spec(g.shape), _const_spec(b.shape)],
        out_specs=tok,
        out_shape=jax.ShapeDtypeStruct((m, D_MODEL), F32),
        scratch_shapes=[pltpu.VMEM((tm + 8, D_MODEL), F32), pltpu.VMEM((1, D_MODEL), F32)],
        compiler_params=_cparams("parallel", "arbitrary"),
        name="rglru_ln",
    )(x, *ws, g, b)


def kernel(x, ln_g, ln_b, ffn_w1, ffn_w3, ffn_w2, ret_w_in, ret_w_out, s5_w_in, s5_a_re, s5_a_im, s5_b_re, s5_b_im, s5_c_re, s5_c_im, s5_d, s5_log_step, s5_w_glu, s5_w_out, rw_mu, rw_w_r, rw_w_k, rw_w_v, rw_w0, rw_w1, rw_w2, rw_a0, rw_a1, rw_a2, rw_g1, rw_g2, rw_k_k, rw_k_a, rw_r_k, rw_lnx_g, rw_lnx_b, rw_w_o, lru_w_in, lru_conv_w, lru_conv_b, lru_w_a, lru_b_a, lru_w_x, lru_b_x, lru_lambda, lru_w_out):
    batch, seq, d = x.shape
    depth = ln_g.shape[0]
    h = x.reshape(batch * seq, d)
    bf = lambda t: t.astype(BF16)
    row = lambda t: t.reshape(1, -1)
    w1, w3, w2 = bf(ffn_w1), bf(ffn_w3), bf(ffn_w2)
    for i in range(depth):
        m, j = i % N_MIXERS, i // N_MIXERS
        h = _ffn_ln(h, w1, w3, w2, i, 0, row(ln_g[i, 0]), row(ln_b[i, 0]))
        g, b = row(ln_g[i, 1]), row(ln_b[i, 1])
        if m == 0:
            h = _retention_ln(h, batch, bf(ret_w_in[j]), bf(ret_w_out[j]), g, b)
        elif m == 1:
            ops = _s5_operators(s5_a_re[j], s5_a_im[j], s5_b_re[j], s5_b_im[j], s5_c_re[j], s5_c_im[j],
                                s5_log_step[j])
            h = _s5_ln(h, batch, bf(s5_w_in[j]), ops, row(s5_d[j]), bf(s5_w_glu[j]), bf(s5_w_out[j]), g, b)
        elif m == 2:
            p = dict(mu=rw_mu[j], w_r=bf(rw_w_r[j]), w_k=bf(rw_w_k[j]), w_v=bf(rw_w_v[j]),
                     w0=row(rw_w0[j]), w1=bf(rw_w1[j]), w2=bf(rw_w2[j]),
                     a0=row(rw_a0[j]), a1=bf(rw_a1[j]), a2=bf(rw_a2[j]), g1=bf(rw_g1[j]), g2=bf(rw_g2[j]),
                     k_k=row(rw_k_k[j]), k_a=row(rw_k_a[j]), r_k=row(rw_r_k[j]),
                     lnx_g=row(rw_lnx_g[j]), lnx_b=row(rw_lnx_b[j]), w_o=bf(rw_w_o[j]))
            h = _rwkv_ln(h, batch, p, g, b)
        else:
            p = dict(w_in=bf(lru_w_in[j]), conv_w=lru_conv_w[j], conv_b=row(lru_conv_b[j]),
                     w_a=bf(lru_w_a[j]), b_a=row(lru_b_a[j]), w_x=bf(lru_w_x[j]), b_x=row(lru_b_x[j]),
                     lam=row(lru_lambda[j]), w_out=bf(lru_w_out[j]))
            h = _lru_ln(h, batch, p, g, b)
        h = _ffn_ln(h, w1, w3, w2, i, 1, row(ln_g[i, 2]), row(ln_b[i, 2]))
    return h.reshape(batch, seq, d)
```

```python
import functools
import math

import jax
import jax.numpy as jnp
from jax import lax
from jax.experimental import pallas as pl
from jax.experimental.pallas import tpu as pltpu

F32 = jnp.float32
BF16 = jnp.bfloat16

D_MODEL = 1024
DEPTH = 4
N_MIXERS = 4
ALPHA = (2 * DEPTH) ** 0.25
LN_EPS = 1e-5
D_FF = 2816
FFN_HALF = 0.5

RET_HEADS = 4
RET_DK = 256
RET_DV = 512
ROPE_BASE = 10000.0
RET_GN_EPS = 1e-5

S5_GROUPS = 64
S5_GROUP = 16
S5_STATE = 64
S5_SUB = 16
S5_GB = 16
S5_NGB = S5_GROUPS // S5_GB

RW_HEADS = 16
RW_HEAD = 64
RW_GN_EPS = 64e-5
RW_CHUNK = 64
RW_SUM_LANES = 256
RW_SCAN_TILE = 256

LRU_BLOCKS = 4
LRU_BLOCK = 256
CONV_WIDTH = 4
LRU_C = 8.0
LRU_ROWS = 8

VMEM_LIMIT = 56 * 1024 * 1024


def _cparams(*sem):
    return pltpu.CompilerParams(dimension_semantics=sem, vmem_limit_bytes=VMEM_LIMIT)


def _const_spec(shape):
    nd = len(shape)
    return pl.BlockSpec(shape, lambda *_: (0,) * nd, pipeline_mode=pl.Buffered(1))


def _layer_norm(r, g, b):
    mu = jnp.mean(r, axis=-1, keepdims=True)
    d = r - mu
    var = jnp.mean(d * d, axis=-1, keepdims=True)
    return d * lax.rsqrt(var + LN_EPS) * g + b


def _dot(a, b):
    return jnp.dot(a, b, preferred_element_type=F32)


def _dot_bf(a, b):
    return jnp.dot(a.astype(BF16), b.astype(BF16), preferred_element_type=F32)


def _softplus(z):
    return jnp.maximum(z, 0.0) + jnp.log(1.0 + jnp.exp(-jnp.abs(z)))


def _ffn_kernel(x_ref, w1_ref, w3_ref, w2_ref, g_ref, b_ref, o_ref):
    x = x_ref[...]
    xb = x.astype(BF16)
    h1 = _dot(xb, w1_ref[...])
    h3 = _dot(xb, w3_ref[...])
    act = (h1 * jax.nn.sigmoid(h1) * h3).astype(BF16)
    y = _dot(act, w2_ref[...])
    o_ref[...] = _layer_norm(ALPHA * x + FFN_HALF * y, g_ref[...], b_ref[...])


def _ffn_ln(x, w1, w3, w2, layer, slot, g, b, tm=512):
    m = x.shape[0]
    tm = min(tm, m)
    tok = pl.BlockSpec((tm, D_MODEL), lambda i: (i, 0))
    pick = lambda w: pl.BlockSpec((None, None) + w.shape[2:], lambda i: (layer, slot, 0, 0),
                                  pipeline_mode=pl.Buffered(1))
    return pl.pallas_call(
        _ffn_kernel,
        grid=(m // tm,),
        in_specs=[tok, pick(w1), pick(w3), pick(w2), _const_spec(g.shape), _const_spec(b.shape)],
        out_specs=tok,
        out_shape=jax.ShapeDtypeStruct((m, D_MODEL), F32),
        compiler_params=_cparams("parallel"),
        name="ffn_ln",
    )(x, w1, w3, w2, g, b)


def _rotate(t1, t2, cos, sin):
    return jnp.concatenate([t1 * cos - t2 * sin, t1 * sin + t2 * cos], axis=-1)


def _ret_kernel(x_ref, cos_ref, sin_ref, win_ref, wout_ref, dmat_ref, qd_ref, kd_ref,
                g_ref, b_ref, o_ref, state_ref):
    @pl.when(pl.program_id(1) == 0)
    def _():
        state_ref[...] = jnp.zeros_like(state_ref)

    H, DK, DV = RET_HEADS, RET_DK, RET_DV
    half = DK // 2
    x = x_ref[...]
    tc = x.shape[0]
    proj = _dot(x.astype(BF16), win_ref[...])
    cos = cos_ref[...]
    sin = sin_ref[...]
    y = jnp.zeros_like(x)
    for h in range(H):
        q0 = h * DK
        k0 = H * DK + h * DK
        v0 = 2 * H * DK + h * DV
        g0 = 2 * H * DK + H * DV + h * DV
        qh = _rotate(proj[:, q0:q0 + half], proj[:, q0 + half:q0 + DK], cos, sin)
        kh = _rotate(proj[:, k0:k0 + half], proj[:, k0 + half:k0 + DK], cos, sin) * (DK ** -0.5)
        vh = proj[:, v0:v0 + DV].astype(BF16)
        gate = proj[:, g0:g0 + DV]
        s = lax.dot_general(qh.astype(BF16), kh.astype(BF16), (((1,), (1,)), ((), ())),
                            preferred_element_type=F32) * dmat_ref[h]
        state = state_ref[h]
        o = _dot(s.astype(BF16), vh) + _dot_bf(qh * qd_ref[h], state)
        kdec = (kh * kd_ref[h]).T.astype(BF16)
        chunk_decay = math.exp(tc * math.log1p(-(2.0 ** (-5.0 - h))))
        state_ref[h] = state * chunk_decay + _dot(kdec, vh)
        mu = jnp.mean(o, axis=-1, keepdims=True)
        d = o - mu
        var = jnp.mean(d * d, axis=-1, keepdims=True)
        o = d * lax.rsqrt(var + RET_GN_EPS)
        o = gate * jax.nn.sigmoid(gate) * o
        y = y + _dot(o.astype(BF16), wout_ref[h * DV:(h + 1) * DV, :])
    o_ref[...] = _layer_norm(ALPHA * x + y, g_ref[...], b_ref[...])


def _retention_ln(x, batch, w_in, w_out, g, b, tc=512):
    m = x.shape[0]
    seq = m // batch
    tc = min(tc, seq)
    nt = seq // tc
    H, DK = RET_HEADS, RET_DK
    half = DK // 2
    inv = ROPE_BASE ** (-jnp.arange(half, dtype=F32) / half)
    ang = jnp.arange(seq, dtype=F32)[:, None] * inv[None, :]
    cos, sin = jnp.cos(ang), jnp.sin(ang)
    log_gamma = jnp.log1p(-jnp.power(2.0, -5.0 - jnp.arange(H, dtype=F32)))
    pos = jnp.arange(tc, dtype=F32)
    rel = pos[:, None] - pos[None, :]
    dmat = jnp.where(rel >= 0, jnp.exp(jnp.maximum(rel, 0.0)[None] * log_gamma[:, None, None]), 0.0)
    qd = jnp.broadcast_to(jnp.exp((pos + 1.0)[None, :, None] * log_gamma[:, None, None]), (H, tc, DK))
    kd = jnp.broadcast_to(jnp.exp((tc - 1.0 - pos)[None, :, None] * log_gamma[:, None, None]), (H, tc, DK))

    tok = pl.BlockSpec((tc, D_MODEL), lambda bi, ti: (bi * nt + ti, 0))
    rope = pl.BlockSpec((tc, half), lambda bi, ti: (ti, 0))
    return pl.pallas_call(
        _ret_kernel,
        grid=(batch, nt),
        in_specs=[tok, rope, rope, _const_spec(w_in.shape), _const_spec(w_out.shape),
                  _const_spec(dmat.shape), _const_spec(qd.shape), _const_spec(kd.shape),
                  _const_spec(g.shape), _const_spec(b.shape)],
        out_specs=tok,
        out_shape=jax.ShapeDtypeStruct((m, D_MODEL), F32),
        scratch_shapes=[pltpu.VMEM((H, DK, RET_DV), F32)],
        compiler_params=_cparams("parallel", "arbitrary"),
        name="retention_ln",
    )(x, cos, sin, w_in, w_out, dmat, qd, kd, g, b)


def _s5_kernel(x_ref, perm_ref, permt_ref, win_ref, bre_ref, bim_ref, cre_ref, cim_ref, pwr_ref, pwi_ref, d_ref,
               wglu_ref, wout_ref, g_ref, b_ref, o_ref, hre_ref, him_ref, ere_ref, eim_ref,
               sre_ref, sim_ref, y_ref):
    @pl.when(pl.program_id(1) == 0)
    def _():
        sre_ref[...] = jnp.zeros_like(sre_ref)
        sim_ref[...] = jnp.zeros_like(sim_ref)

    L = S5_SUB
    tm = x_ref.shape[0]
    nr = tm // L
    cols = S5_GB * S5_GROUP
    x = x_ref[...]
    xp = _dot(perm_ref[...], x.astype(BF16)).astype(BF16)
    u = _dot(xp, win_ref[...])
    ub = u.astype(BF16)
    for gb in range(S5_NGB):
        ug = ub[:, gb * cols:(gb + 1) * cols]
        hre_ref[...] = _dot(ug, bre_ref[gb])
        him_ref[...] = _dot(ug, bim_ref[gb])
        sub = lambda j: slice(j * nr, (j + 1) * nr)
        ar = pwr_ref[gb, 1:2, :]
        ai = pwi_ref[gb, 1:2, :]
        hr = hre_ref[sub(0), :]
        hi = him_ref[sub(0), :]
        for j in range(1, L):
            hr, hi = (ar * hr - ai * hi + hre_ref[sub(j), :], ar * hi + ai * hr + him_ref[sub(j), :])
            hre_ref[sub(j), :] = hr
            him_ref[sub(j), :] = hi
        ere_ref[...] = hr
        eim_ref[...] = hi
        alr = pwr_ref[gb, L:L + 1, :]
        ali = pwi_ref[gb, L:L + 1, :]

        def row(c, carry):
            sr, si = carry
            er = ere_ref[pl.ds(c, 1), :]
            ei = eim_ref[pl.ds(c, 1), :]
            ere_ref[pl.ds(c, 1), :] = sr
            eim_ref[pl.ds(c, 1), :] = si
            return alr * sr - ali * si + er, alr * si + ali * sr + ei

        sr, si = lax.fori_loop(0, nr, row, (sre_ref[gb], sim_ref[gb]))
        sre_ref[gb] = sr
        sim_ref[gb] = si
        pr_in = ere_ref[...]
        pi_in = eim_ref[...]
        for j in range(L):
            pr = pwr_ref[gb, j + 1:j + 2, :]
            pi = pwi_ref[gb, j + 1:j + 2, :]
            hre_ref[sub(j), :] = hre_ref[sub(j), :] + (pr * pr_in - pi * pi_in)
            him_ref[sub(j), :] = him_ref[sub(j), :] + (pr * pi_in + pi * pr_in)
        y_ref[:, gb * cols:(gb + 1) * cols] = (_dot_bf(hre_ref[...], cre_ref[gb])
                                               + _dot_bf(him_ref[...], cim_ref[gb]))
    yv = y_ref[...] + d_ref[...] * u
    act = jax.nn.gelu(yv)
    z = act * jax.nn.sigmoid(_dot(act.astype(BF16), wglu_ref[...]))
    out = _dot(z.astype(BF16), wout_ref[...])
    hi16 = out.astype(BF16)
    lo16 = (out - hi16.astype(F32)).astype(BF16)
    out = _dot(permt_ref[...], hi16) + _dot(permt_ref[...], lo16)
    o_ref[...] = _layer_norm(ALPHA * x + out, g_ref[...], b_ref[...])


def _s5_zoh_kernel(are_ref, aim_ref, ls_ref, bre_ref, bim_ref, bbre_ref, bbim_ref, pwr_ref, pwi_ref):
    a_re = are_ref[...]
    a_im = aim_ref[...]
    dt = jnp.exp(ls_ref[...])
    mag = jnp.exp(dt * a_re)
    abar_re = mag * jnp.cos(dt * a_im)
    abar_im = mag * jnp.sin(dt * a_im)
    den = a_re * a_re + a_im * a_im
    f_re = (((abar_re - 1.0) * a_re + abar_im * a_im) / den)[:, None, :]
    f_im = ((abar_im * a_re - (abar_re - 1.0) * a_im) / den)[:, None, :]
    b_re = bre_ref[...]
    b_im = bim_ref[...]
    bbre_ref[...] = f_re * b_re - f_im * b_im
    bbim_ref[...] = f_re * b_im + f_im * b_re
    pr = jnp.ones_like(abar_re)
    pi = jnp.zeros_like(abar_re)
    pwr_ref[0] = pr
    pwi_ref[0] = pi
    for l in range(1, S5_SUB + 1):
        pr, pi = pr * abar_re - pi * abar_im, pr * abar_im + pi * abar_re
        pwr_ref[l] = pr
        pwi_ref[l] = pi


def _s5_operators(a_re, a_im, b_re, b_im, c_re, c_im, log_step):
    L = S5_SUB
    G, N, P = b_re.shape
    vm = pl.BlockSpec(memory_space=pltpu.VMEM)
    bb_re, bb_im, pw_re, pw_im = pl.pallas_call(
        _s5_zoh_kernel,
        in_specs=[vm] * 5,
        out_specs=[vm] * 4,
        out_shape=[jax.ShapeDtypeStruct((G, P, N), F32)] * 2 + [jax.ShapeDtypeStruct((L + 1, G, N), F32)] * 2,
        name="s5_zoh",
    )(a_re, a_im, log_step.reshape(G, 1), b_re.transpose(0, 2, 1), b_im.transpose(0, 2, 1))
    gbn, ngb = S5_GB, S5_NGB
    eye = jnp.eye(gbn, dtype=F32)
    blockdiag_in = lambda t: jnp.einsum('bgpn,gh->bgphn', t.reshape(ngb, gbn, P, N), eye).reshape(
        ngb, gbn * P, gbn * N)
    blockdiag_out = lambda t: jnp.einsum('bgpn,gh->bgnhp', t.reshape(ngb, gbn, P, N), eye).reshape(
        ngb, gbn * N, gbn * P)
    planes = lambda t: t.reshape(L + 1, ngb, gbn * N).transpose(1, 0, 2)
    return (blockdiag_in(bb_re).astype(BF16), blockdiag_in(bb_im).astype(BF16),
            blockdiag_out(c_re).astype(BF16), blockdiag_out(-c_im).astype(BF16), planes(pw_re), planes(pw_im))


def _s5_ln(x, batch, w_in, ops, d_skip, w_glu, w_out, g, b, tm=512):
    m = x.shape[0]
    seq = m // batch
    tm = min(tm, seq)
    nt = seq // tm
    nr = tm // S5_SUB
    rows = jnp.arange(tm)
    perm = (rows[None, :] == (rows[:, None] % nr) * S5_SUB + rows[:, None] // nr).astype(BF16)
    ws = [perm, perm.T, w_in, *ops, d_skip, w_glu, w_out, g, b]
    tok = pl.BlockSpec((tm, D_MODEL), lambda bi, ti: (bi * nt + ti, 0))
    wide = S5_GB * S5_STATE
    return pl.pallas_call(
        _s5_kernel,
        grid=(batch, nt),
        in_specs=[tok] + [_const_spec(w.shape) for w in ws],
        out_specs=tok,
        out_shape=jax.ShapeDtypeStruct((m, D_MODEL), F32),
        scratch_shapes=[pltpu.VMEM((tm, wide), F32), pltpu.VMEM((tm, wide), F32),
                        pltpu.VMEM((nr, wide), F32), pltpu.VMEM((nr, wide), F32),
                        pltpu.VMEM((S5_NGB, 1, wide), F32), pltpu.VMEM((S5_NGB, 1, wide), F32),
                        pltpu.VMEM((tm, D_MODEL), F32)],
        compiler_params=_cparams("parallel", "arbitrary"),
        name="s5_ln",
    )(x, *ws)


def _rw_proj_kernel(x_ref, xp_ref, mu_ref, wr_ref, wk_ref, wv_ref, w0_ref, w1_ref, w2_ref,
                    a0_ref, a1_ref, a2_ref, g1_ref, g2_ref,
                    r_ref, k_ref, v_ref, lw_ref, a_ref, g_ref, *, tiles_per_seq):
    x = x_ref[...]
    first = (pl.program_id(0) % tiles_per_seq) == 0
    prev = jnp.where(first, 0.0, xp_ref[7:8, :])
    rows = lax.broadcasted_iota(jnp.int32, x.shape, 0)
    shifted = jnp.where(rows == 0, prev, pltpu.roll(x, 1, 0))
    xx = shifted - x
    mix = lambda i: (x + xx * mu_ref[i:i + 1, :]).astype(BF16)
    lw1 = _dot(mix(1), w1_ref[...])
    la1 = _dot(mix(4), a1_ref[...])
    lg1 = _dot(mix(5), g1_ref[...])
    r_ref[...] = _dot(mix(0), wr_ref[...])
    k_ref[...] = _dot(mix(2), wk_ref[...])
    z = w0_ref[...] + _dot_bf(jnp.tanh(lw1), w2_ref[...])
    w = -_softplus(-z) - 0.5
    lw_ref[...] = -jnp.exp(w)
    a_ref[...] = jax.nn.sigmoid(a0_ref[...] + _dot_bf(la1, a2_ref[...]))
    g_ref[...] = _dot_bf(jax.nn.sigmoid(lg1), g2_ref[...])
    v_ref[...] = _dot(mix(3), wv_ref[...])


def _rw_scan_kernel(r_ref, k_ref, v_ref, lw_ref, a_ref, kk_ref, ka_ref, rk_ref, lg_ref, lb_ref,
                    mask_ref, ones_ref, bd_ref, x_ref, gate_ref, wo_ref, g_ref, b_ref, o_ref, state_ref):
    @pl.when(pl.program_id(1) == 0)
    def _():
        state_ref[...] = jnp.zeros_like(state_ref)

    L, N = RW_CHUNK, RW_HEAD
    PW = 2 * N
    tm, width = r_ref.shape
    nc, npair = tm // L, width // PW
    nt = (((1,), (1,)), ((), ()))
    tn = (((0,), (0,)), ((), ()))
    strict = mask_ref[0]
    incl = mask_ref[1]
    rr = lax.broadcasted_iota(jnp.int32, (L, PW), 0)
    cc = lax.broadcasted_iota(jnp.int32, (L, PW), 1)
    eye = jnp.where(rr == (cc & (N - 1)), 1.0, 0.0).astype(F32)
    first = cc < N
    bd = bd_ref[...]

    def bdiag(t):
        zero = jnp.zeros_like(t)
        return jnp.concatenate([jnp.where(first, t, zero), jnp.where(first, zero, t)], axis=0)

    lw = lw_ref[...]
    pos = lax.broadcasted_iota(jnp.int32, lw.shape, 0) & (L - 1)
    cum = lw
    step = 1
    while step < L:
        cum = cum + jnp.where(pos >= step, pltpu.roll(cum, step, 0), 0.0)
        step *= 2
    g_in = jnp.exp(cum)
    g_ex = jnp.exp(cum - lw)
    g_inv = jnp.exp(-cum)
    r = r_ref[...]
    k = k_ref[...]
    a = a_ref[...]

    def head_sum(t):
        hw = ones_ref.shape[0]
        hi = t.astype(BF16)
        lo = (t - hi.astype(F32)).astype(BF16)
        return jnp.concatenate(
            [_dot(hi[:, j:j + hw], ones_ref[...]) + _dot(lo[:, j:j + hw], ones_ref[...])
             for j in range(0, t.shape[1], hw)], axis=-1)

    kkr = k * kk_ref[...]
    kk_all = kkr / jnp.maximum(jnp.sqrt(head_sum(kkr * kkr)), 1e-12)
    k = k * (1.0 + (a - 1.0) * ka_ref[...])
    bonus = head_sum(r * k * rk_ref[...]) * v_ref[...]
    at_all = (-kk_all * g_ex).astype(BF16)
    bt_all = kk_all * a * g_inv
    rt_all = (r * g_in).astype(BF16)
    kt_all = k * g_inv
    vb_all = v_ref[...].astype(BF16)

    chains = [(p, c) for p in range(npair) for c in range(nc)]
    sl = lambda t, i: t[i[1] * L:(i[1] + 1) * L, i[0] * PW:(i[0] + 1) * PW]
    atc = {i: sl(at_all, i) for i in chains}
    rtc = {i: sl(rt_all, i) for i in chains}
    ktc = {i: sl(kt_all, i) for i in chains}
    btc = {i: sl(bt_all, i) for i in chains}
    vbd = {i: bdiag(sl(vb_all, i)) for i in chains}
    gl = {i: g_in[(i[1] + 1) * L - 1:(i[1] + 1) * L, i[0] * PW:(i[0] + 1) * PW] for i in chains}
    ar = {i: jnp.concatenate([atc[i], rtc[i]], axis=0) for i in chains}
    gram = {i: lax.dot_general(ar[i], jnp.concatenate([bdiag(btc[i].astype(BF16)),
                                                       bdiag(ktc[i].astype(BF16))], axis=0),
                               nt, preferred_element_type=F32) for i in chains}
    a_ab = {i: gram[i][:L, :PW] * strict for i in chains}
    b_rb = {i: (gram[i][L:, :PW] * incl).astype(BF16) for i in chains}
    a_ak = {i: (gram[i][:L, PW:] * strict).astype(BF16) for i in chains}
    b_rk = {i: (gram[i][L:, PW:] * incl).astype(BF16) for i in chains}
    tinv = {i: eye + a_ab[i] * mask_ref[2] for i in chains}
    lvl, bsz = 1, 2
    while bsz < L:
        mask = mask_ref[2 + lvl]
        half = {i: _dot((a_ab[i] * mask).astype(BF16), bdiag(tinv[i].astype(BF16))) for i in chains}
        tinv = {i: tinv[i] + _dot(tinv[i].astype(BF16), bdiag(half[i].astype(BF16))) for i in chains}
        bsz *= 2
        lvl += 1
    tb = {i: tinv[i].astype(BF16) for i in chains}
    av = {i: _dot(jnp.concatenate([a_ak[i], b_rk[i]], axis=0), vbd[i]) for i in chains}
    bv = {i: av[i][L:] for i in chains}
    tw = {i: _dot(tb[i], jnp.concatenate([bdiag(atc[i]), bdiag(av[i][:L].astype(BF16))], axis=1))
          for i in chains}
    ap = {i: tw[i][:, :PW].astype(BF16) for i in chains}
    wloc = {i: tw[i][:, PW:] for i in chains}
    bg = {i: (btc[i] * gl[i]).astype(BF16) for i in chains}
    kv = {i: lax.dot_general(sl(vb_all, i), (ktc[i] * gl[i]).astype(BF16), tn,
                             preferred_element_type=F32) * bd for i in chains}
    apr = {i: jnp.concatenate([ap[i], rtc[i]], axis=0) for i in chains}

    states = [state_ref[p] for p in range(npair)]
    outs = [[] for _ in range(npair)]
    for c in range(nc):
        both = [lax.dot_general(apr[p, c], states[p].astype(BF16), nt, preferred_element_type=F32)
                for p in range(npair)]
        ub = [(both[p][:L] + wloc[p, c]).astype(BF16) for p in range(npair)]
        for p in range(npair):
            states[p] = (states[p] * gl[p, c] + kv[p, c]
                         + lax.dot_general(ub[p], bg[p, c], tn, preferred_element_type=F32) * bd)
        for p in range(npair):
            outs[p].append(both[p][L:] + _dot(b_rb[p, c], bdiag(ub[p])) + bv[p, c])

    for p in range(npair):
        state_ref[p] = states[p]
    o = jnp.concatenate([jnp.concatenate(outs[p], axis=0) for p in range(npair)], axis=-1)
    mu = head_sum(o) * (1.0 / N)
    d = o - mu
    var = head_sum(d * d) * (1.0 / N)
    y = d * lax.rsqrt(var + RW_GN_EPS) * lg_ref[...] + lb_ref[...] + bonus
    out = _dot((y * gate_ref[...]).astype(BF16), wo_ref[...])
    o_ref[...] = _layer_norm(ALPHA * x_ref[...] + out, g_ref[...], b_ref[...])


def _rw_masks():
    L = RW_CHUNK
    i = jnp.arange(L)
    r, c = i[:, None], i[None, :]
    masks = [r > c, r >= c]
    bsz = 1
    while bsz < L:
        masks.append((r // (2 * bsz) == c // (2 * bsz)) & ((r // bsz) % 2 == 1) & ((c // bsz) % 2 == 0))
        bsz *= 2
    return jnp.tile(jnp.stack(masks, 0).astype(F32), (1, 1, 2))


def _rwkv_ln(x, batch, p, g, b, tm=512):
    m = x.shape[0]
    seq = m // batch
    tm = min(tm, seq)
    nt = seq // tm
    tok = pl.BlockSpec((tm, D_MODEL), lambda i: (i, 0))
    prev = pl.BlockSpec((8, D_MODEL), lambda i: (jnp.maximum(i * (tm // 8) - 1, 0), 0))
    names = ['mu', 'w_r', 'w_k', 'w_v', 'w0', 'w1', 'w2', 'a0', 'a1', 'a2', 'g1', 'g2']
    ws = [p[n] for n in names]
    act = jax.ShapeDtypeStruct((m, D_MODEL), F32)
    r, k, v, lw, a, gate = pl.pallas_call(
        functools.partial(_rw_proj_kernel, tiles_per_seq=nt),
        grid=(m // tm,),
        in_specs=[tok, prev] + [_const_spec(w.shape) for w in ws],
        out_specs=[tok] * 6,
        out_shape=[act] * 6,
        compiler_params=_cparams("parallel"),
        name="rwkv_proj",
    )(x, x, *ws)

    ts = min(RW_SCAN_TILE, seq)
    nts = seq // ts
    stok = pl.BlockSpec((ts, D_MODEL), lambda bi, ti: (bi * nts + ti, 0))
    masks = _rw_masks()
    head_of_lane = jnp.arange(RW_SUM_LANES) // RW_HEAD
    ones = (head_of_lane[:, None] == head_of_lane[None, :]).astype(BF16)
    bd = ones[:2 * RW_HEAD, :2 * RW_HEAD].astype(F32)
    consts = [p['k_k'], p['k_a'], p['r_k'], p['lnx_g'], p['lnx_b'], masks, ones, bd]
    tail = [p['w_o'], g, b]
    return pl.pallas_call(
        _rw_scan_kernel,
        grid=(batch, nts),
        in_specs=([stok] * 5 + [_const_spec(c.shape) for c in consts] + [stok, stok]
                  + [_const_spec(c.shape) for c in tail]),
        out_specs=stok,
        out_shape=act,
        scratch_shapes=[pltpu.VMEM((RW_HEADS // 2, 2 * RW_HEAD, 2 * RW_HEAD), F32)],
        compiler_params=_cparams("parallel", "arbitrary"),
        name="rwkv_scan_ln",
    )(r, k, v, lw, a, *consts, x, gate, *tail)


def _lru_kernel(x_ref, win_ref, cw_ref, cb_ref, wa_ref, ba_ref, wx_ref, bx_ref, lam_ref, wout_ref,
                g_ref, b_ref, o_ref, xr_ref, h_ref):
    W = D_MODEL
    tm = x_ref.shape[0]

    @pl.when(pl.program_id(1) == 0)
    def _():
        xr_ref[...] = jnp.zeros_like(xr_ref)
        h_ref[...] = jnp.zeros_like(h_ref)

    x = x_ref[...]
    proj = _dot(x.astype(BF16), win_ref[...])
    gate = jax.nn.gelu(proj[:, :W])
    xr_ref[0:8, :] = xr_ref[tm:tm + 8, :]
    xr_ref[8:tm + 8, :] = proj[:, W:]
    xc = cb_ref[...] + cw_ref[3:4, :] * xr_ref[8:tm + 8, :]
    for j in range(CONV_WIDTH - 1):
        xc = xc + cw_ref[j:j + 1, :] * xr_ref[pl.ds(5 + j, tm), :]
    xcb = xc.astype(BF16)
    gr = jnp.concatenate([_dot(xcb[:, kb * LRU_BLOCK:(kb + 1) * LRU_BLOCK], wa_ref[kb])
                          for kb in range(LRU_BLOCKS)], axis=-1) + ba_ref[...]
    gi = jnp.concatenate([_dot(xcb[:, kb * LRU_BLOCK:(kb + 1) * LRU_BLOCK], wx_ref[kb])
                          for kb in range(LRU_BLOCKS)], axis=-1) + bx_ref[...]
    log_a = -LRU_C * jax.nn.sigmoid(gr) * _softplus(-lam_ref[...])
    av = jnp.exp(log_a)
    bv = jnp.sqrt(1.0 - av * av) * (jax.nn.sigmoid(gi) * xc)
    pos = lax.broadcasted_iota(jnp.int32, av.shape, 0) & (LRU_ROWS - 1)
    step = 1
    while step < LRU_ROWS:
        keep = pos >= step
        bv = av * jnp.where(keep, pltpu.roll(bv, step, 0), 0.0) + bv
        av = av * jnp.where(keep, pltpu.roll(av, step, 0), 1.0)
        step *= 2
    nb = tm // LRU_ROWS
    a_end = av.reshape(nb, LRU_ROWS, W)[:, LRU_ROWS - 1, :]
    b_end = bv.reshape(nb, LRU_ROWS, W)[:, LRU_ROWS - 1, :]
    h = h_ref[...]
    carries = []
    for i in range(nb):
        carries.append(h)
        h = a_end[i:i + 1] * h + b_end[i:i + 1]
    h_ref[...] = h
    carry = jnp.concatenate(carries, axis=0)
    hs = av * jnp.broadcast_to(carry[:, None, :], (nb, LRU_ROWS, W)).reshape(tm, W) + bv
    y = _dot((hs * gate).astype(BF16), wout_ref[...])
    o_ref[...] = _layer_norm(ALPHA * x + y, g_ref[...], b_ref[...])


def _lru_ln(x, batch, p, g, b, tm=512):
    m = x.shape[0]
    seq = m // batch
    tm = min(tm, seq)
    nt = seq // tm
    tok = pl.BlockSpec((tm, D_MODEL), lambda bi, ti: (bi * nt + ti, 0))
    names = ['w_in', 'conv_w', 'conv_b', 'w_a', 'b_a', 'w_x', 'b_x', 'lam', 'w_out']
    ws = [p[n] for n in names]
    return pl.pallas_call(
        _lru_kernel,
        grid=(batch, nt),
        in_specs=[tok] + [_const_spec(w.shape) for w in ws] + [_const_spec(g.shape), _const_spec(b.shape)],
        out_specs=tok,
        out_shape=jax.ShapeDtypeStruct((m, D_MODEL), F32),
        scratch_shapes=[pltpu.VMEM((tm + 8, D_MODEL), F32), pltpu.VMEM((1, D_MODEL), F32)],
        compiler_params=_cparams("parallel", "arbitrary"),
        name="rglru_ln",
    )(x, *ws, g, b)


def kernel(x, ln_g, ln_b, ffn_w1, ffn_w3, ffn_w2, ret_w_in, ret_w_out, s5_w_in, s5_a_re, s5_a_im, s5_b_re, s5_b_im, s5_c_re, s5_c_im, s5_d, s5_log_step, s5_w_glu, s5_w_out, rw_mu, rw_w_r, rw_w_k, rw_w_v, rw_w0, rw_w1, rw_w2, rw_a0, rw_a1, rw_a2, rw_g1, rw_g2, rw_k_k, rw_k_a, rw_r_k, rw_lnx_g, rw_lnx_b, rw_w_o, lru_w_in, lru_conv_w, lru_conv_b, lru_w_a, lru_b_a, lru_w_x, lru_b_x, lru_lambda, lru_w_out):
    batch, seq, d = x.shape
    depth = ln_g.shape[0]
    h = x.reshape(batch * seq, d)
    bf = lambda t: t.astype(BF16)
    row = lambda t: t.reshape(1, -1)
    w1, w3, w2 = bf(ffn_w1), bf(ffn_w3), bf(ffn_w2)
    for i in range(depth):
        m, j = i % N_MIXERS, i // N_MIXERS
        h = _ffn_ln(h, w1, w3, w2, i, 0, row(ln_g[i, 0]), row(ln_b[i, 0]))
        g, b = row(ln_g[i, 1]), row(ln_b[i, 1])
        if m == 0:
            h = _retention_ln(h, batch, bf(ret_w_in[j]), bf(ret_w_out[j]), g, b)
        elif m == 1:
            ops = _s5_operators(s5_a_re[j], s5_a_im[j], s5_b_re[j], s5_b_im[j], s5_c_re[j], s5_c_im[j],
                                s5_log_step[j])
            h = _s5_ln(h, batch, bf(s5_w_in[j]), ops, row(s5_d[j]), bf(s5_w_glu[j]), bf(s5_w_out[j]), g, b)
        elif m == 2:
            p = dict(mu=rw_mu[j], w_r=bf(rw_w_r[j]), w_k=bf(rw_w_k[j]), w_v=bf(rw_w_v[j]),
                     w0=row(rw_w0[j]), w1=bf(rw_w1[j]), w2=bf(rw_w2[j]),
                     a0=row(rw_a0[j]), a1=bf(rw_a1[j]), a2=bf(rw_a2[j]), g1=bf(rw_g1[j]), g2=bf(rw_g2[j]),
                     k_k=row(rw_k_k[j]), k_a=row(rw_k_a[j]), r_k=row(rw_r_k[j]),
                     lnx_g=row(rw_lnx_g[j]), lnx_b=row(rw_lnx_b[j]), w_o=bf(rw_w_o[j]))
            h = _rwkv_ln(h, batch, p, g, b)
        else:
            p = dict(w_in=bf(lru_w_in[j]), conv_w=lru_conv_w[j], conv_b=row(lru_conv_b[j]),
                     w_a=bf(lru_w_a[j]), b_a=row(lru_b_a[j]), w_x=bf(lru_w_x[j]), b_x=row(lru_b_x[j]),
                     lam=row(lru_lambda[j]), w_out=bf(lru_w_out[j]))
            h = _lru_ln(h, batch, p, g, b)
        h = _ffn_ln(h, w1, w3, w2, i, 1, row(ln_g[i, 2]), row(ln_b[i, 2]))
    return h.reshape(batch, seq, d)
```
